```python
import jax, jax.numpy as jnp
from jax import lax
import numpy as np

D_MODEL = 1024
BATCH = 4
SEQ = 8192
DEPTH = 2
DEC_BATCH = 32
DEC_SEQ = 8
PAST_LEN = 16384
PAGE_SIZE = 128

HEAD_DIM = 64
N_HEADS_A = D_MODEL // 2 // HEAD_DIM
CONV_CH = D_MODEL // 2
CONV_WIDTH = 31
N_IDX_HEADS = 4
IDX_DIM = 64
TOPK_MAX = 256
Q_BLOCK = 128
N_HEADS_C = D_MODEL // HEAD_DIM
DILATED_PATTERNS = ((128, 1), (512, 4), (2048, 16))
MAX_WINDOW = 2048
N_EXPERTS = 64
TOP_K = 6
EXPERT_DIM = 256
SHARED_DIM = 256
ROUTED_SCALE = 2.5
MOE_BLOCK = 128
ROPE_THETA = 10000.0
EPS = 1e-6
N_EVEN = (DEPTH + 1) // 2
N_ODD = DEPTH // 2
EVEN_SPLIT = [N_HEADS_A * HEAD_DIM] * 3 + [N_IDX_HEADS * IDX_DIM, IDX_DIM, N_IDX_HEADS, CONV_CH, CONV_CH]
IN_EVEN = sum(EVEN_SPLIT)

kernel_name = 'dsa_conformer_dilated_moe_decoder_step'


def rms_norm(x, g):
    xf = x.astype(jnp.float32)
    y = xf * lax.rsqrt(jnp.mean(xf * xf, axis=-1, keepdims=True) + EPS)
    return (y * g.astype(jnp.float32)).astype(x.dtype)


def rope(x, pos):
    half = x.shape[-1] // 2
    inv = ROPE_THETA ** (-jnp.arange(half, dtype=jnp.float32) / half)
    ang = pos.astype(jnp.float32)[:, None] * inv[None, :]
    cos = jnp.cos(ang)[None, :, None, :]
    sin = jnp.sin(ang)[None, :, None, :]
    xf = x.astype(jnp.float32)
    x1, x2 = xf[..., :half], xf[..., half:]
    return jnp.concatenate([x1 * cos - x2 * sin, x2 * cos + x1 * sin], axis=-1).astype(x.dtype)


def adaln(c, w, b):
    mod = (jax.nn.silu(c) @ w + b)[:, None, :]
    return jnp.split(mod, 6, axis=-1)


def modulate(h, shift, scale):
    return h * (1 + scale) + shift


def tail_rows(x, n):
    t = x.shape[1]
    if t < n:
        x = jnp.pad(x, ((0, 0), (n - t, 0)) + ((0, 0),) * (x.ndim - 2))
    return x[:, -n:]


def gather_rows(x, idx):
    return jax.vmap(lambda a, i: a[i])(x, idx)


def even_project(h, pos, w_in, qn_g, kn_g):
    B, T, _ = h.shape
    z = h @ w_in
    cuts = [int(c) for c in np.cumsum(EVEN_SPLIT)[:-1]]
    q, k, v, qi, ki, wi, ua, ub = jnp.split(z, cuts, axis=-1)
    q = rope(rms_norm(q.reshape(B, T, N_HEADS_A, HEAD_DIM), qn_g), pos)
    k = rope(rms_norm(k.reshape(B, T, N_HEADS_A, HEAD_DIM), kn_g), pos)
    v = v.reshape(B, T, N_HEADS_A, HEAD_DIM)
    qi = rope(qi.reshape(B, T, N_IDX_HEADS, IDX_DIM), pos)
    ki = rope(ki[:, :, None, :], pos)[:, :, 0]
    wi = wi * (N_IDX_HEADS ** -0.5 * IDX_DIM ** -0.5)
    u = ua * jax.nn.sigmoid(ub)
    return q, k, v, qi, ki, wi, u


def index_scores(qi, ki, wi):
    r = jax.nn.relu(jnp.einsum('bqhd,bkd->bqhk', qi, ki))
    return jnp.einsum('bqhk,bqh->bqk', r, wi).astype(jnp.float32)


def sparse_attend(q, kg, vg, valid):
    s = jnp.einsum('bqhd,bqkhd->bqhk', q, kg).astype(jnp.float32) * (HEAD_DIM ** -0.5)
    s = jnp.where(valid[:, :, None, :], s, -jnp.inf)
    p = jax.nn.softmax(s, axis=-1).astype(vg.dtype)
    return jnp.einsum('bqhk,bqkhd->bqhd', p, vg)


def dsa_prompt(q, k, v, qi, ki, wi):
    B, S = q.shape[0], q.shape[1]
    topk = min(TOPK_MAX, S // 4)
    nb = S // Q_BLOCK
    kpos = jnp.arange(S, dtype=jnp.int32)

    def block(args):
        qb, qib, wib, start = args
        qpos = start + jnp.arange(Q_BLOCK, dtype=jnp.int32)
        sc = index_scores(qib, ki, wib)
        sc = jnp.where((kpos[None, :] <= qpos[:, None])[None], sc, -jnp.inf)
        _, sel = lax.top_k(sc, topk)
        valid = sel <= qpos[None, :, None]
        return sparse_attend(qb, gather_rows(k, sel), gather_rows(v, sel), valid)

    def blocks(x):
        return jnp.swapaxes(x.reshape(B, nb, Q_BLOCK, *x.shape[2:]), 0, 1)

    starts = jnp.arange(nb, dtype=jnp.int32) * Q_BLOCK
    out = lax.map(block, (blocks(q), blocks(qi), blocks(wi), starts))
    return jnp.swapaxes(out, 0, 1).reshape(B, S, N_HEADS_A, HEAD_DIM)


def dsa_sample(q, k_new, v_new, qi, ki_new, wi, pool_k, pool_v, pool_ki, page_table):
    DB, DS = q.shape[0], q.shape[1]
    page = pool_k.shape[1]
    past = page_table.shape[1] * page
    L = past + DS
    topk = min(TOPK_MAX, L // 4)
    ki_past = pool_ki[page_table].reshape(DB, past, IDX_DIM)
    ki_all = jnp.concatenate([ki_past, ki_new], axis=1)
    sc = index_scores(qi, ki_all, wi)
    qpos = past + jnp.arange(DS, dtype=jnp.int32)
    kpos = jnp.arange(L, dtype=jnp.int32)
    sc = jnp.where((kpos[None, :] <= qpos[:, None])[None], sc, -jnp.inf)
    _, sel = lax.top_k(sc, topk)
    valid = sel <= qpos[None, :, None]
    in_past = (sel < past)[..., None, None]
    sp = jnp.minimum(sel, past - 1)
    phys = jax.vmap(lambda pt, s: pt[s])(page_table, sp // page) * page + sp % page
    sn = jnp.clip(sel - past, 0, DS - 1)
    kg = jnp.where(in_past, pool_k.reshape(-1, N_HEADS_A, HEAD_DIM)[phys], gather_rows(k_new, sn))
    vg = jnp.where(in_past, pool_v.reshape(-1, N_HEADS_A, HEAD_DIM)[phys], gather_rows(v_new, sn))
    return sparse_attend(q, kg, vg, valid)


def conv_module(u, buf, conv_w, conv_b, ln_g, ln_b):
    xp = jnp.concatenate([buf, u], axis=1)
    y = lax.conv_general_dilated(xp, conv_w[:, None, :], window_strides=(1,), padding='VALID',
                                 dimension_numbers=('NWC', 'WIO', 'NWC'),
                                 feature_group_count=u.shape[-1]) + conv_b
    yf = y.astype(jnp.float32)
    mu = jnp.mean(yf, axis=-1, keepdims=True)
    var = jnp.mean(jnp.square(yf - mu), axis=-1, keepdims=True)
    yn = (yf - mu) * lax.rsqrt(var + EPS) * ln_g.astype(jnp.float32) + ln_b.astype(jnp.float32)
    return jax.nn.silu(yn).astype(u.dtype), xp[:, -(CONV_WIDTH - 1):]


def odd_project(h, pos, w_in, qn_g, kn_g):
    B, T, _ = h.shape
    q, k, v = jnp.split(h @ w_in, 3, axis=-1)
    q = rope(rms_norm(q.reshape(B, T, N_HEADS_C, HEAD_DIM), qn_g), pos)
    k = rope(rms_norm(k.reshape(B, T, N_HEADS_C, HEAD_DIM), kn_g), pos)
    return q, k, v.reshape(B, T, N_HEADS_C, HEAD_DIM)


def dilated_branch_prompt(q, k, v, window, dil):
    B, S, H, hd = q.shape
    nb = window // dil
    m = S // dil
    mp = -(-m // nb) * nb
    nblk = mp // nb

    def fold(x):
        x = x.reshape(B, m, dil, H, hd).transpose(0, 2, 1, 3, 4).reshape(B * dil, m, H, hd)
        return jnp.pad(x, ((0, 0), (0, mp - m), (0, 0), (0, 0)))

    def band(x):
        prev = jnp.pad(x, ((0, 0), (nb, 0), (0, 0), (0, 0)))[:, :mp].reshape(-1, nblk, nb, H, hd)
        return jnp.concatenate([prev, x.reshape(-1, nblk, nb, H, hd)], axis=2)

    qb = fold(q).reshape(-1, nblk, nb, H, hd)
    kb, vb = band(fold(k)), band(fold(v))
    s = jnp.einsum('nbqhd,nbkhd->nbhqk', qb, kb).astype(jnp.float32) * (hd ** -0.5)
    i = jnp.arange(nb)[:, None]
    j = jnp.arange(2 * nb)[None, :]
    rel = i + nb - j
    blk = jnp.arange(nblk)[:, None, None]
    ok = (rel >= 0) & (rel <= nb) & (blk * nb - nb + j >= 0)
    s = jnp.where(ok[None, :, None], s, -jnp.inf)
    lse = jax.nn.logsumexp(s, axis=-1)
    p = jnp.exp(s - lse[..., None]).astype(v.dtype)
    o = jnp.einsum('nbhqk,nbkhd->nbqhd', p, vb)

    def unfold(x):
        tail = x.shape[3:]
        x = x.reshape(B, dil, mp, *tail)[:, :, :m]
        return jnp.moveaxis(x, 1, 2).reshape(B, S, *tail)

    return unfold(o), unfold(lse.transpose(0, 1, 3, 2))


def dilated_branch_sample(q, kall, vall, window, dil, buf_len):
    DS = q.shape[1]
    idx = buf_len + jnp.arange(DS)[:, None] - jnp.arange(window // dil + 1)[None, :] * dil
    ok = idx >= 0
    idxc = jnp.maximum(idx, 0)
    kg, vg = kall[:, idxc], vall[:, idxc]
    s = jnp.einsum('bqhd,bqkhd->bqhk', q, kg).astype(jnp.float32) * (HEAD_DIM ** -0.5)
    s = jnp.where(ok[None, :, None, :], s, -jnp.inf)
    lse = jax.nn.logsumexp(s, axis=-1)
    p = jnp.exp(s - lse[..., None]).astype(vg.dtype)
    return jnp.einsum('bqhk,bqkhd->bqhd', p, vg), lse


def combine_branches(outs, lses):
    w = jax.nn.softmax(jnp.stack(lses, axis=0), axis=0)
    o = jnp.stack(outs, axis=0)
    return jnp.sum(w[..., None].astype(o.dtype) * o, axis=0)


def swiglu(x, wg, wu, wd):
    return (jax.nn.silu(x @ wg) * (x @ wu)) @ wd


def routed_experts(x, idx, gate, wg, wu, wd):
    N, D = x.shape
    A = N * TOP_K
    e = idx.reshape(-1)
    tok = jnp.repeat(jnp.arange(N, dtype=jnp.int32), TOP_K)
    order = jnp.argsort(e)
    se, stok, sgate = e[order], tok[order], gate.reshape(-1)[order]
    counts = jnp.bincount(e, length=N_EXPERTS)
    starts = jnp.cumsum(counts) - counts
    pcounts = (counts + MOE_BLOCK - 1) // MOE_BLOCK * MOE_BLOCK
    pends = jnp.cumsum(pcounts)
    pstarts = pends - pcounts
    dest = pstarts[se] + jnp.arange(A) - starts[se]
    cap = -(-A // MOE_BLOCK) * MOE_BLOCK + N_EXPERTS * MOE_BLOCK
    nblk = cap // MOE_BLOCK
    xp = jnp.zeros((cap, D), x.dtype).at[dest].set(x[stok])
    blk_e = jnp.minimum(jnp.searchsorted(pends, jnp.arange(nblk) * MOE_BLOCK, side='right'), N_EXPERTS - 1)

    def block(args):
        xb, eb = args
        return swiglu(xb, wg[eb], wu[eb], wd[eb])

    yp = lax.map(block, (xp.reshape(nblk, MOE_BLOCK, D), blk_e)).reshape(cap, D)
    return jnp.zeros((N, D), x.dtype).at[stok].add(yp[dest] * sgate[:, None].astype(x.dtype))


def moe(h, w_router, b_router, wg, wu, wd, wgs, wus, wds):
    B, T, D = h.shape
    x = h.reshape(B * T, D)
    s = jax.nn.sigmoid((x @ w_router).astype(jnp.float32))
    _, idx = lax.top_k(s + b_router.astype(jnp.float32), TOP_K)
    g = jnp.take_along_axis(s, idx, axis=-1)
    g = g / jnp.sum(g, axis=-1, keepdims=True) * ROUTED_SCALE
    y = routed_experts(x, idx, g, wg, wu, wd) + swiglu(x, wgs, wus, wds)
    return y.reshape(B, T, D)


def setup_inputs(seed: int = 0) -> dict:
    key = jax.random.key(seed)
    ks = iter(jax.random.split(key, 40))
    f32 = jnp.float32

    def nrm(shape, scale=1.0):
        return jax.random.normal(next(ks), shape, f32) * scale

    n_pages = PAST_LEN // PAGE_SIZE
    used = DEC_BATCH * n_pages
    n_phys = used + max(1, used // 4)
    win_buf = min(MAX_WINDOW, PAST_LEN)
    D = D_MODEL
    inp = {}
    inp['x_prompt'] = nrm((BATCH, SEQ, D))
    inp['x_sample'] = nrm((DEC_BATCH, DEC_SEQ, D))
    inp['c_prompt'] = nrm((BATCH, D))
    inp['c_sample'] = nrm((DEC_BATCH, D))
    inp['cache_a_k'] = nrm((N_EVEN, n_phys, PAGE_SIZE, N_HEADS_A, HEAD_DIM))
    inp['cache_a_v'] = nrm((N_EVEN, n_phys, PAGE_SIZE, N_HEADS_A, HEAD_DIM))
    inp['cache_a_idx_k'] = nrm((N_EVEN, n_phys, PAGE_SIZE, IDX_DIM))
    inp['state_conv'] = nrm((N_EVEN, DEC_BATCH, CONV_WIDTH - 1, CONV_CH))
    inp['cache_c_k'] = nrm((N_ODD, DEC_BATCH, win_buf, N_HEADS_C, HEAD_DIM))
    inp['cache_c_v'] = nrm((N_ODD, DEC_BATCH, win_buf, N_HEADS_C, HEAD_DIM))
    inp['page_table'] = jax.random.permutation(next(ks), n_phys)[:used].reshape(DEC_BATCH, n_pages).astype(jnp.int32)
    inp['norm_mix_g'] = 1.0 + nrm((DEPTH, D), 0.02)
    inp['norm_ffn_g'] = 1.0 + nrm((DEPTH, D), 0.02)
    inp['w_ada'] = nrm((DEPTH, D, 6 * D), 0.5 * D ** -0.5)
    inp['b_ada'] = nrm((DEPTH, 6 * D), 0.01)
    inp['w_in_even'] = nrm((N_EVEN, D, IN_EVEN), D ** -0.5)
    inp['w_out_even'] = nrm((N_EVEN, N_HEADS_A * HEAD_DIM + CONV_CH, D), (N_HEADS_A * HEAD_DIM + CONV_CH) ** -0.5)
    inp['a_q_norm_g'] = 1.0 + nrm((N_EVEN, HEAD_DIM), 0.02)
    inp['a_k_norm_g'] = 1.0 + nrm((N_EVEN, HEAD_DIM), 0.02)
    inp['b_conv_w'] = nrm((N_EVEN, CONV_WIDTH, CONV_CH), CONV_WIDTH ** -0.5)
    inp['b_conv_b'] = nrm((N_EVEN, CONV_CH), 0.01)
    inp['b_ln_g'] = 1.0 + nrm((N_EVEN, CONV_CH), 0.02)
    inp['b_ln_b'] = nrm((N_EVEN, CONV_CH), 0.01)
    inp['w_in_odd'] = nrm((N_ODD, D, 3 * N_HEADS_C * HEAD_DIM), D ** -0.5)
    inp['w_out_odd'] = nrm((N_ODD, N_HEADS_C * HEAD_DIM, D), (N_HEADS_C * HEAD_DIM) ** -0.5)
    inp['c_q_norm_g'] = 1.0 + nrm((N_ODD, HEAD_DIM), 0.02)
    inp['c_k_norm_g'] = 1.0 + nrm((N_ODD, HEAD_DIM), 0.02)
    inp['w_router'] = nrm((DEPTH, D, N_EXPERTS), D ** -0.5)
    inp['b_router'] = nrm((DEPTH, N_EXPERTS), 0.01)
    inp['w_gate_e'] = nrm((DEPTH, N_EXPERTS, D, EXPERT_DIM), D ** -0.5)
    inp['w_up_e'] = nrm((DEPTH, N_EXPERTS, D, EXPERT_DIM), D ** -0.5)
    inp['w_down_e'] = nrm((DEPTH, N_EXPERTS, EXPERT_DIM, D), EXPERT_DIM ** -0.5)
    inp['w_gate_s'] = nrm((DEPTH, D, SHARED_DIM), D ** -0.5)
    inp['w_up_s'] = nrm((DEPTH, D, SHARED_DIM), D ** -0.5)
    inp['w_down_s'] = nrm((DEPTH, SHARED_DIM, D), SHARED_DIM ** -0.5)
    return inp


def reference(x_prompt, x_sample, c_prompt, c_sample, cache_a_k, cache_a_v, cache_a_idx_k, state_conv,
              cache_c_k, cache_c_v, page_table, norm_mix_g, norm_ffn_g, w_ada, b_ada, w_in_even, w_out_even,
              a_q_norm_g, a_k_norm_g, b_conv_w, b_conv_b, b_ln_g, b_ln_b, w_in_odd, w_out_odd, c_q_norm_g,
              c_k_norm_g, w_router, b_router, w_gate_e, w_up_e, w_down_e, w_gate_s, w_up_s, w_down_s):
    past = page_table.shape[1] * cache_a_k.shape[2]
    win_buf = cache_c_k.shape[2]
    bp, sp_len = x_prompt.shape[0], x_prompt.shape[1]
    bs, ss_len = x_sample.shape[0], x_sample.shape[1]
    pos_p = jnp.arange(sp_len, dtype=jnp.int32)
    pos_s = past + jnp.arange(ss_len, dtype=jnp.int32)
    xp, xs = x_prompt, x_sample
    ak_p, av_p, ai_p, ak_s, av_s, ai_s, cv_p, cv_s = [], [], [], [], [], [], [], []
    ck_p, cw_p, ck_s, cw_s = [], [], [], []
    for layer in range(DEPTH):
        mod_p = adaln(c_prompt, w_ada[layer], b_ada[layer])
        mod_s = adaln(c_sample, w_ada[layer], b_ada[layer])
        hp = modulate(rms_norm(xp, norm_mix_g[layer]), mod_p[0], mod_p[1])
        hs = modulate(rms_norm(xs, norm_mix_g[layer]), mod_s[0], mod_s[1])
        if layer % 2 == 0:
            e = layer // 2
            q, k, v, qi, ki, wi, u = even_project(hp, pos_p, w_in_even[e], a_q_norm_g[e], a_k_norm_g[e])
            oa = dsa_prompt(q, k, v, qi, ki, wi)
            ob, conv_new = conv_module(u, jnp.zeros((bp, CONV_WIDTH - 1, CONV_CH), u.dtype),
                                       b_conv_w[e], b_conv_b[e], b_ln_g[e], b_ln_b[e])
            mix_p = jnp.concatenate([oa.reshape(bp, sp_len, -1), ob], axis=-1) @ w_out_even[e]
            ak_p.append(k); av_p.append(v); ai_p.append(ki); cv_p.append(conv_new)
            q, k, v, qi, ki, wi, u = even_project(hs, pos_s, w_in_even[e], a_q_norm_g[e], a_k_norm_g[e])
            oa = dsa_sample(q, k, v, qi, ki, wi, cache_a_k[e], cache_a_v[e], cache_a_idx_k[e], page_table)
            ob, conv_new = conv_module(u, state_conv[e], b_conv_w[e], b_conv_b[e], b_ln_g[e], b_ln_b[e])
            mix_s = jnp.concatenate([oa.reshape(bs, ss_len, -1), ob], axis=-1) @ w_out_even[e]
            ak_s.append(k); av_s.append(v); ai_s.append(ki); cv_s.append(conv_new)
        else:
            o = layer // 2
            q, k, v = odd_project(hp, pos_p, w_in_odd[o], c_q_norm_g[o], c_k_norm_g[o])
            br = [dilated_branch_prompt(q, k, v, w, d) for (w, d) in DILATED_PATTERNS]
            att = combine_branches([b[0] for b in br], [b[1] for b in br])
            mix_p = att.reshape(bp, sp_len, -1) @ w_out_odd[o]
            ck_p.append(tail_rows(k, win_buf)); cw_p.append(tail_rows(v, win_buf))
            q, k, v = odd_project(hs, pos_s, w_in_odd[o], c_q_norm_g[o], c_k_norm_g[o])
            kall = jnp.concatenate([cache_c_k[o], k], axis=1)
            vall = jnp.concatenate([cache_c_v[o], v], axis=1)
            br = [dilated_branch_sample(q, kall, vall, w, d, win_buf) for (w, d) in DILATED_PATTERNS]
            att = combine_branches([b[0] for b in br], [b[1] for b in br])
            mix_s = att.reshape(bs, ss_len, -1) @ w_out_odd[o]
            ck_s.append(kall[:, -win_buf:]); cw_s.append(vall[:, -win_buf:])
        xp = xp + mod_p[2] * mix_p
        xs = xs + mod_s[2] * mix_s
        hp = modulate(rms_norm(xp, norm_ffn_g[layer]), mod_p[3], mod_p[4])
        hs = modulate(rms_norm(xs, norm_ffn_g[layer]), mod_s[3], mod_s[4])
        xp = xp + mod_p[5] * moe(hp, w_router[layer], b_router[layer], w_gate_e[layer], w_up_e[layer],
                                 w_down_e[layer], w_gate_s[layer], w_up_s[layer], w_down_s[layer])
        xs = xs + mod_s[5] * moe(hs, w_router[layer], b_router[layer], w_gate_e[layer], w_up_e[layer],
                                 w_down_e[layer], w_gate_s[layer], w_up_s[layer], w_down_s[layer])
    return (xp, xs,
            jnp.stack(ak_p), jnp.stack(av_p), jnp.stack(ai_p),
            jnp.stack(ak_s), jnp.stack(av_s), jnp.stack(ai_s),
            jnp.stack(cv_p), jnp.stack(cv_s),
            jnp.stack(ck_p), jnp.stack(cw_p), jnp.stack(ck_s), jnp.stack(cw_s))
```

```python
import functools

import numpy as np
import jax
import jax.numpy as jnp
from jax import lax
from jax.experimental import pallas as pl
from jax.experimental.pallas import tpu as pltpu

F32, BF16, I32 = jnp.float32, jnp.bfloat16, jnp.int32

HEAD_DIM = 64
N_HEADS_A = 8
CONV_CH = 512
CONV_WIDTH = 31
N_IDX_HEADS = 4
IDX_DIM = 64
TOPK_MAX = 256
N_HEADS_C = 16
DILATED_PATTERNS = ((128, 1), (512, 4), (2048, 16))
N_EXPERTS = 64
TOP_K = 6
ROUTED_SCALE = 2.5
ROPE_THETA = 10000.0
EPS = 1e-6

LANE = 128
NEG = -1e30
KEY_NEG_INF = -2139095041
VMEM_LIMIT = 60 * 1024 * 1024

A_W = N_HEADS_A * HEAD_DIM
C_W = N_HEADS_C * HEAD_DIM


def _cparams(sem):
    return pltpu.CompilerParams(dimension_semantics=sem, vmem_limit_bytes=VMEM_LIMIT)


def _dot(a, b):
    return jnp.dot(a, b, preferred_element_type=F32)


def _dot_nt(a, b):
    return lax.dot_general(a, b, (((1,), (1,)), ((), ())), preferred_element_type=F32)


def _dot_tn(a, b):
    return lax.dot_general(a, b, (((0,), (0,)), ((), ())), preferred_element_type=F32)


def _silu(x):
    return x * jax.nn.sigmoid(x)


def _rms_mod(x, g, shift, scale):
    ms = jnp.mean(x * x, axis=-1, keepdims=True)
    y = x * lax.rsqrt(ms + EPS) * g
    return y * (1.0 + scale) + shift


def _to_key(x):
    b = lax.bitcast_convert_type(x, I32)
    k = jnp.where(b < 0, b ^ 0x7FFFFFFF, b)
    return jnp.where(k == -1, 0, k)


def _rope_tables(pos):
    half = HEAD_DIM // 2
    inv = ROPE_THETA ** (-jnp.arange(half, dtype=F32) / half)
    ang = pos.astype(F32)[:, None] * inv[None, :]
    c, s = jnp.cos(ang), jnp.sin(ang)
    z = jnp.zeros_like(s)
    cos = jnp.tile(c, (1, 4))
    sin_a = jnp.tile(jnp.concatenate([z, s], axis=1), (1, 2))
    sin_b = jnp.tile(jnp.concatenate([-s, z], axis=1), (1, 2))
    return cos, sin_a, sin_b


def _head_sum_matrix():
    r = np.arange(LANE)
    return jnp.asarray((r[:, None] // HEAD_DIM == r[None, :] // HEAD_DIM).astype(np.float32), BF16)


def _prefix_matrix(n):
    r = np.arange(n)
    return jnp.asarray((r[:, None] <= r[None, :]).astype(np.float32), BF16)


def _mult_bias(rel):
    mult = np.zeros(rel.shape, np.int64)
    for w, d in DILATED_PATTERNS:
        mult += ((rel >= 0) & (rel <= w) & (rel % d == 0))
    return np.where(mult > 0, np.log(np.maximum(mult, 1)), NEG).astype(np.float32)


def _ada_kernel(c_ref, w_ref, b_ref, o_ref):
    a = _silu(c_ref[...]).astype(BF16)
    o_ref[...] = _dot(a, w_ref[...].astype(BF16)) + b_ref[...]


def _adaln(c, w, b):
    m, d = c.shape
    n = w.shape[1]
    tn = 1536
    return pl.pallas_call(
        _ada_kernel, grid=(n // tn,),
        in_specs=[pl.BlockSpec((m, d), lambda j: (0, 0)),
                  pl.BlockSpec((d, tn), lambda j: (0, j)),
                  pl.BlockSpec((1, tn), lambda j: (0, j))],
        out_specs=pl.BlockSpec((m, tn), lambda j: (0, j)),
        out_shape=jax.ShapeDtypeStruct((m, n), F32),
        compiler_params=_cparams(("arbitrary",)), name="adaln")(c, w, b.reshape(1, n))


def _norm_rope(z, bsum, g, cos, sin_a, sin_b, norm):
    if norm:
        sq = z * z
        hi = sq.astype(BF16)
        lo = (sq - hi.astype(F32)).astype(BF16)
        ms = (_dot(hi, bsum) + _dot(lo, bsum)) * (1.0 / HEAD_DIM)
        z = z * lax.rsqrt(ms + EPS) * g
    return z * cos + pltpu.roll(z, 32, 1) * sin_a + pltpu.roll(z, 96, 1) * sin_b


def _proj_even_kernel(x_ref, gn_ref, sh_ref, sc_ref, w_ref, cos_ref, sa_ref, sb_ref, qg_ref, kg_ref, bsum_ref,
                      q_ref, k_ref, kb_ref, v_ref, vb_ref, qi_ref, ki_ref, kib_ref, wi_ref, u_ref, h_scr):
    h_scr[...] = _rms_mod(x_ref[...], gn_ref[...], sh_ref[...], sc_ref[...]).astype(BF16)
    hb = h_scr[...]
    cos, sa, sb, bsum = cos_ref[...], sa_ref[...], sb_ref[...], bsum_ref[...]

    def seg(lo, n):
        return _dot(hb, w_ref[:, lo:lo + n])

    z = seg(0, A_W)
    for c in range(A_W // LANE):
        sl = slice(c * LANE, (c + 1) * LANE)
        q_ref[:, sl] = _norm_rope(z[:, sl], bsum, qg_ref[...], cos, sa, sb, True).astype(BF16)
    z = seg(A_W, A_W)
    for c in range(A_W // LANE):
        sl = slice(c * LANE, (c + 1) * LANE)
        kk = _norm_rope(z[:, sl], bsum, kg_ref[...], cos, sa, sb, True)
        k_ref[:, sl] = kk
        kb_ref[:, sl] = kk.astype(BF16)
    z = seg(2 * A_W, A_W)
    v_ref[...] = z
    vb_ref[...] = z.astype(BF16)
    z = seg(3 * A_W, 2 * LANE)
    for c in range(2):
        sl = slice(c * LANE, (c + 1) * LANE)
        qi_ref[:, sl] = _norm_rope(z[:, sl], bsum, None, cos, sa, sb, False).astype(BF16)
    z = _norm_rope(seg(3 * A_W + 2 * LANE, LANE), bsum, None, cos, sa, sb, False)
    ki_ref[...] = z
    kib_ref[...] = z.astype(BF16)
    off = 3 * A_W + 3 * LANE
    ua = seg(off, CONV_CH)
    ub = seg(off + CONV_CH, CONV_CH)
    u_ref[...] = ua * jax.nn.sigmoid(ub)
    wi_ref[...] = seg(off + 2 * CONV_CH, LANE) * (N_IDX_HEADS ** -0.5 * IDX_DIM ** -0.5)


def _proj_odd_kernel(x_ref, gn_ref, sh_ref, sc_ref, w_ref, cos_ref, sa_ref, sb_ref, qg_ref, kg_ref, bsum_ref,
                     q_ref, k_ref, kb_ref, v_ref, vb_ref, h_scr):
    h_scr[...] = _rms_mod(x_ref[...], gn_ref[...], sh_ref[...], sc_ref[...]).astype(BF16)
    hb = h_scr[...]
    cos, sa, sb, bsum = cos_ref[...], sa_ref[...], sb_ref[...], bsum_ref[...]
    for half in range(2):
        z = _dot(hb, w_ref[:, half * A_W:(half + 1) * A_W])
        for c in range(A_W // LANE):
            sl = slice(c * LANE, (c + 1) * LANE)
            dl = slice(half * A_W + c * LANE, half * A_W + (c + 1) * LANE)
            q_ref[:, dl] = _norm_rope(z[:, sl], bsum, qg_ref[...], cos, sa, sb, True).astype(BF16)
    for half in range(2):
        z = _dot(hb, w_ref[:, C_W + half * A_W:C_W + (half + 1) * A_W])
        for c in range(A_W // LANE):
            sl = slice(c * LANE, (c + 1) * LANE)
            dl = slice(half * A_W + c * LANE, half * A_W + (c + 1) * LANE)
            kk = _norm_rope(z[:, sl], bsum, kg_ref[...], cos, sa, sb, True)
            k_ref[:, dl] = kk
            kb_ref[:, dl] = kk.astype(BF16)
    for half in range(2):
        z = _dot(hb, w_ref[:, 2 * C_W + half * A_W:2 * C_W + (half + 1) * A_W])
        v_ref[:, half * A_W:(half + 1) * A_W] = z
        vb_ref[:, half * A_W:(half + 1) * A_W] = z.astype(BF16)


def _mod_spec(mod, tr, tiles_per_group):
    d = mod.shape[-1]
    if mod.ndim == 3:
        return pl.BlockSpec((None, 1, d), lambda i: (i // tiles_per_group, 0, 0))
    return pl.BlockSpec((tr, d), lambda i: (i, 0))


def _row_tile(rows, want):
    return want if rows % want == 0 else rows


def _project(kern, x, gn, shift, scale, w, tables, qg, kg, group_rows, outs):
    r, d = x.shape
    tr = _row_tile(min(r, group_rows), 512)
    tpg = max(group_rows // tr, 1)
    tpt = tables[0].shape[0] // tr
    const = lambda a: pl.BlockSpec(a.shape, lambda i: (0,) * a.ndim)
    tab = pl.BlockSpec((tr, LANE), lambda i: (i % tpt, 0))
    bsum = _head_sum_matrix()
    return pl.pallas_call(
        kern, grid=(r // tr,),
        in_specs=[pl.BlockSpec((tr, d), lambda i: (i, 0)), const(gn), _mod_spec(shift, tr, tpg),
                  _mod_spec(scale, tr, tpg), const(w), tab, tab, tab, const(qg), const(kg), const(bsum)],
        out_specs=[pl.BlockSpec((tr, wd), lambda i: (i, 0)) for wd, _ in outs],
        out_shape=[jax.ShapeDtypeStruct((r, wd), dt) for wd, dt in outs],
        scratch_shapes=[pltpu.VMEM((tr, d), BF16)],
        compiler_params=_cparams(("arbitrary",)), name=kern.__name__.strip("_"),
    )(x, gn, shift, scale, w, *tables, qg, kg, bsum)


def _perm_w_even(w):
    d = w.shape[0]
    cuts = np.cumsum([A_W, A_W, A_W, N_IDX_HEADS * IDX_DIM, IDX_DIM, N_IDX_HEADS, CONV_CH])
    q, k, v, qi, ki, wi, ua, ub = jnp.split(w, [int(c) for c in cuts], axis=1)
    pad = jnp.zeros((d, LANE - N_IDX_HEADS), w.dtype)
    return jnp.concatenate([q, k, v, qi, ki, ki, ua, ub, wi, pad], axis=1).astype(BF16)


def _dsa_prompt_kernel(q_ref, qi_ref, wi_ref, ki_ref, k_ref, v_ref, u_ref, o_ref, key_ref, bias_ref,
                       *, tq, kc, topk):
    i = pl.program_id(1)
    q0 = i * tq
    cd = q0 // kc
    lane = lax.broadcasted_iota(I32, (tq, LANE), 1)
    lo_half = lane < HEAD_DIM
    zero = jnp.zeros((tq, LANE), BF16)

    qi = qi_ref[...]
    qm = [jnp.where(lo_half if h % 2 == 0 else ~lo_half, qi[:, (h // 2) * LANE:(h // 2 + 1) * LANE], zero)
          for h in range(N_IDX_HEADS)]
    wi = wi_ref[...]
    wcol = [wi[:, h:h + 1] for h in range(N_IDX_HEADS)]

    def score_chunk(c):
        kk = ki_ref[pl.ds(pl.multiple_of(c * kc, kc), kc), :]
        acc = None
        for h in range(N_IDX_HEADS):
            t = wcol[h] * jnp.maximum(_dot_nt(qm[h], kk), 0.0)
            acc = t if acc is None else acc + t
        return acc

    def fill(c, _):
        key_ref[c] = _to_key(score_chunk(c))
        return 0

    lax.fori_loop(0, cd, fill, 0)
    kpos = cd * kc + lax.broadcasted_iota(I32, (tq, kc), 1)
    qpos = q0 + lax.broadcasted_iota(I32, (tq, kc), 0)
    key_ref[cd] = _to_key(jnp.where(kpos <= qpos, score_chunk(cd), -jnp.inf))

    def count(cand, strict):
        def body(c, cnt):
            kk = key_ref[c]
            one = jnp.where((kk > cand) if strict else (kk >= cand), 1.0, 0.0)
            part = one[:, 0:LANE]
            for j in range(1, kc // LANE):
                part = part + one[:, j * LANE:(j + 1) * LANE]
            return cnt + part
        cnt = lax.fori_loop(0, cd + 1, body, jnp.zeros((tq, LANE), F32))
        return jnp.sum(cnt, axis=1, keepdims=True)

    kf = float(topk)
    thr = jnp.where(count(jnp.zeros((tq, 1), I32), False) >= kf, 0, -2 ** 31).astype(I32)

    def bit(b, thr):
        cand = thr + jnp.left_shift(jnp.int32(1), 30 - b)
        return jnp.where(count(cand, False) >= kf, cand, thr)

    thr = lax.fori_loop(0, 31, bit, thr)
    need = kf - count(thr, True)
    upper = u_ref[...]

    def select(c, run):
        kk = key_ref[c]
        tie = kk == thr
        pre = _dot(jnp.where(tie, 1.0, 0.0).astype(BF16), upper) + run
        sel = ((kk > thr) | (tie & (pre <= need))) & (kk > KEY_NEG_INF)
        bias_ref[c] = jnp.where(sel, 0.0, NEG)
        return pre[:, kc - 1:kc]

    lax.fori_loop(0, cd + 1, select, jnp.zeros((tq, 1), F32))

    for hp in range(A_W // LANE):
        cols = slice(hp * LANE, (hp + 1) * LANE)
        qp = q_ref[:, cols]
        outs = []
        for sub in range(2):
            qh = jnp.where(lo_half if sub == 0 else ~lo_half, qp, zero)

            def body(c, carry, qh=qh):
                m, l, acc = carry
                rows = pl.ds(pl.multiple_of(c * kc, kc), kc)
                s = _dot_nt(qh, k_ref[rows, cols]) * (HEAD_DIM ** -0.5) + bias_ref[c]
                mn = jnp.maximum(m, jnp.max(s, axis=1, keepdims=True))
                p = jnp.exp(s - mn)
                a = jnp.exp(m - mn)
                l = a * l + jnp.sum(p, axis=1, keepdims=True)
                acc = a * acc + _dot(p.astype(BF16), v_ref[rows, cols])
                return mn, l, acc

            init = (jnp.full((tq, 1), NEG, F32), jnp.zeros((tq, 1), F32), jnp.zeros((tq, LANE), F32))
            _, l, acc = lax.fori_loop(0, cd + 1, body, init)
            outs.append(acc / l)
        o_ref[:, cols] = jnp.where(lo_half, outs[0], outs[1]).astype(o_ref.dtype)


def _dsa_prompt(q, qi, wi, ki2, k, v):
    b, s, _ = q.shape
    tq = 128
    kc = min(512, s)
    topk = min(TOPK_MAX, s // 4)
    nc = s // kc
    upper = _prefix_matrix(kc)
    qspec = lambda w: pl.BlockSpec((None, tq, w), lambda bi, i: (bi, i, 0))
    kspec = lambda w: pl.BlockSpec((None, s, w), lambda bi, i: (bi, 0, 0))
    return pl.pallas_call(
        functools.partial(_dsa_prompt_kernel, tq=tq, kc=kc, topk=topk),
        grid=(b, s // tq),
        in_specs=[qspec(A_W), qspec(2 * LANE), qspec(LANE), kspec(LANE), kspec(A_W), kspec(A_W),
                  pl.BlockSpec((kc, kc), lambda bi, i: (0, 0))],
        out_specs=qspec(A_W),
        out_shape=jax.ShapeDtypeStruct((b, s, A_W), BF16),
        scratch_shapes=[pltpu.VMEM((nc, tq, kc), I32), pltpu.VMEM((nc, tq, kc), F32)],
        compiler_params=_cparams(("arbitrary", "arbitrary")), name="dsa_prompt",
    )(q, qi, wi, ki2, k, v, upper)


def _dsa_sample_select_kernel(pt_ref, qi_ref, wi_ref, kip_ref, kin_ref, u_ref, o_ref, key_ref, *, n_new, npg, topk):
    del pt_ref
    p = pl.program_id(1)
    qi = qi_ref[...]
    wi = wi_ref[...]

    def score(ki):
        r = jnp.maximum(_dot_nt(qi, ki.astype(BF16)), 0.0) * wi
        acc = r[0:n_new]
        for h in range(1, N_IDX_HEADS):
            acc = acc + r[h * n_new:(h + 1) * n_new]
        return acc

    key_ref[p] = _to_key(score(kip_ref[...]))

    @pl.when(p == npg - 1)
    def _():
        col = lax.broadcasted_iota(I32, (n_new, LANE), 1)
        row = lax.broadcasted_iota(I32, (n_new, LANE), 0)
        key_ref[npg] = _to_key(jnp.where(col <= row, score(kin_ref[...]), -jnp.inf))

        def count(cand, strict):
            def body(c, cnt):
                kk = key_ref[c]
                return cnt + jnp.where((kk > cand) if strict else (kk >= cand), 1.0, 0.0)
            cnt = lax.fori_loop(0, npg + 1, body, jnp.zeros((n_new, LANE), F32))
            return jnp.sum(cnt, axis=1, keepdims=True)

        kf = float(topk)
        thr = jnp.where(count(jnp.zeros((n_new, 1), I32), False) >= kf, 0, -2 ** 31).astype(I32)

        def bit(b, thr):
            cand = thr + jnp.left_shift(jnp.int32(1), 30 - b)
            return jnp.where(count(cand, False) >= kf, cand, thr)

        thr = lax.fori_loop(0, 31, bit, thr)
        need = kf - count(thr, True)
        upper = u_ref[...]

        def select(c, run):
            kk = key_ref[c]
            tie = kk == thr
            pre = _dot(jnp.where(tie, 1.0, 0.0).astype(BF16), upper) + run
            sel = ((kk > thr) | (tie & (pre <= need))) & (kk > KEY_NEG_INF)
            o_ref[c] = jnp.where(sel, 1.0, 0.0)
            return pre[:, LANE - 1:LANE]

        lax.fori_loop(0, npg + 1, select, jnp.zeros((n_new, 1), F32))


def _dsa_sample_attend_kernel(pt_ref, qbd_ref, kp_ref, vp_ref, kn_ref, vn_ref, sel_ref, rep_ref, o_ref,
                              m_ref, l_ref, acc_ref, *, npg):
    del pt_ref
    p = pl.program_id(1)

    @pl.when(p == 0)
    def _():
        m_ref[...] = jnp.full(m_ref.shape, NEG, F32)
        l_ref[...] = jnp.zeros(l_ref.shape, F32)
        acc_ref[...] = jnp.zeros(acc_ref.shape, F32)

    def step(kk, vv, sel):
        keep = _dot_tn(sel.astype(BF16), rep_ref[...])
        s = _dot(kk.astype(BF16), qbd_ref[...]) * (HEAD_DIM ** -0.5) + (keep - 1.0) * (-NEG)
        m = m_ref[...]
        mn = jnp.maximum(m, jnp.max(s, axis=0, keepdims=True))
        pr = jnp.exp(s - mn)
        a = jnp.exp(m - mn)
        l_ref[...] = a * l_ref[...] + jnp.sum(pr, axis=0, keepdims=True)
        acc_ref[...] = a * acc_ref[...] + _dot_tn(vv.astype(BF16), pr.astype(BF16))
        m_ref[...] = mn

    step(kp_ref[...], vp_ref[...], sel_ref[p])

    @pl.when(p == npg - 1)
    def _():
        step(kn_ref[...], vn_ref[...], sel_ref[npg])
        o_ref[...] = acc_ref[...] / l_ref[...]


def _dsa_sample(q, k_new, v_new, qi, ki_new, wi, pool_k, pool_v, pool_ki, page_table):
    db, ds, _ = q.shape
    n_phys, page = pool_k.shape[0], pool_k.shape[1]
    npg = page_table.shape[1]
    topk = min(TOPK_MAX, (npg * page + ds) // 4)
    pt = page_table.reshape(-1).astype(I32)
    hq = N_IDX_HEADS * ds
    qi_r = qi.reshape(db, ds, N_IDX_HEADS, IDX_DIM).transpose(0, 2, 1, 3).reshape(db, hq, IDX_DIM)
    wi_r = jnp.broadcast_to(wi.transpose(0, 2, 1).reshape(db, hq, 1), (db, hq, LANE))
    pad = lambda a: jnp.pad(a, ((0, 0), (0, page - ds), (0, 0)))
    per_b = lambda *shape: pl.BlockSpec((None,) + shape, lambda b, p, pt: (b,) + (0,) * len(shape))
    paged = lambda w: pl.BlockSpec((None, page, w), lambda b, p, pt: (pt[b * npg + p], 0, 0))
    sel = pl.pallas_call(
        functools.partial(_dsa_sample_select_kernel, n_new=ds, npg=npg, topk=topk),
        grid_spec=pltpu.PrefetchScalarGridSpec(
            num_scalar_prefetch=1, grid=(db, npg),
            in_specs=[per_b(hq, IDX_DIM), per_b(hq, LANE), paged(IDX_DIM), per_b(page, IDX_DIM),
                      pl.BlockSpec((page, page), lambda b, p, pt: (0, 0))],
            out_specs=per_b(npg + 1, ds, page),
            scratch_shapes=[pltpu.VMEM((npg + 1, ds, page), I32)]),
        out_shape=jax.ShapeDtypeStruct((db, npg + 1, ds, page), F32),
        compiler_params=_cparams(("arbitrary", "arbitrary")), name="dsa_sample_select",
    )(pt, qi_r, wi_r, pool_ki, pad(ki_new), _prefix_matrix(page))

    cols = N_HEADS_A * ds
    qt = q.reshape(db, ds, N_HEADS_A, HEAD_DIM).transpose(0, 2, 3, 1)
    eye = jnp.eye(N_HEADS_A, dtype=q.dtype)
    qbd = (qt[:, :, :, None, :] * eye[None, :, None, :, None]).reshape(db, A_W, cols)
    rep = jnp.tile(jnp.eye(ds, dtype=BF16), (1, N_HEADS_A))
    acc_t = pl.pallas_call(
        functools.partial(_dsa_sample_attend_kernel, npg=npg),
        grid_spec=pltpu.PrefetchScalarGridSpec(
            num_scalar_prefetch=1, grid=(db, npg),
            in_specs=[per_b(A_W, cols), paged(A_W), paged(A_W), per_b(page, A_W), per_b(page, A_W),
                      per_b(npg + 1, ds, page), pl.BlockSpec((ds, cols), lambda b, p, pt: (0, 0))],
            out_specs=per_b(A_W, cols),
            scratch_shapes=[pltpu.VMEM((1, cols), F32), pltpu.VMEM((1, cols), F32), pltpu.VMEM((A_W, cols), F32)]),
        out_shape=jax.ShapeDtypeStruct((db, A_W, cols), F32),
        compiler_params=_cparams(("arbitrary", "arbitrary")), name="dsa_sample_attend",
    )(pt, qbd, pool_k.reshape(n_phys, page, A_W), pool_v.reshape(n_phys, page, A_W), pad(k_new), pad(v_new), sel, rep)
    a5 = acc_t.reshape(db, N_HEADS_A, HEAD_DIM, N_HEADS_A, ds)
    diag = jnp.diagonal(a5, axis1=1, axis2=3)
    return diag.transpose(0, 2, 3, 1).reshape(db, ds, A_W)


def _conv_kernel(u_ref, halo_ref, buf_ref, w_ref, b_ref, g_ref, bb_ref, o_ref, xp_ref, *, tt):
    i = pl.program_id(1)

    @pl.when(i == 0)
    def _():
        xp_ref[0:32, :] = buf_ref[...]

    @pl.when(i > 0)
    def _():
        xp_ref[0:32, :] = halo_ref[...]

    xp_ref[32:32 + tt, :] = u_ref[...]
    w = w_ref[...]
    acc = jnp.zeros((tt, CONV_CH), F32) + b_ref[...]
    for j in range(CONV_WIDTH):
        acc = acc + xp_ref[2 + j:2 + j + tt, :] * w[j:j + 1, :]
    mu = jnp.mean(acc, axis=-1, keepdims=True)
    var = jnp.mean(jnp.square(acc - mu), axis=-1, keepdims=True)
    yn = (acc - mu) * lax.rsqrt(var + EPS) * g_ref[...] + bb_ref[...]
    o_ref[...] = _silu(yn).astype(o_ref.dtype)


def _conv_module(u, buf30, conv_w, conv_b, ln_g, ln_b):
    b, t, c = u.shape
    tt = _row_tile(t, 512)
    buf = jnp.pad(buf30, ((0, 0), (2, 0), (0, 0)))
    halo_src = u if t >= 32 else buf
    hb = tt // 32
    w = jnp.pad(conv_w, ((0, 32 - CONV_WIDTH), (0, 0)))
    const = lambda a: pl.BlockSpec(a.shape, lambda bi, i: (0,) * a.ndim)
    row = lambda a: a.reshape(1, c)
    return pl.pallas_call(
        functools.partial(_conv_kernel, tt=tt), grid=(b, t // tt),
        in_specs=[pl.BlockSpec((None, tt, c), lambda bi, i: (bi, i, 0)),
                  pl.BlockSpec((None, 32, c), lambda bi, i: (bi, jnp.maximum(i * hb - 1, 0), 0)),
                  pl.BlockSpec((None, 32, c), lambda bi, i: (bi, 0, 0)),
                  const(w), const(row(conv_b)), const(row(ln_g)), const(row(ln_b))],
        out_specs=pl.BlockSpec((None, tt, c), lambda bi, i: (bi, i, 0)),
        out_shape=jax.ShapeDtypeStruct((b, t, c), BF16),
        scratch_shapes=[pltpu.VMEM((32 + tt, c), F32)],
        compiler_params=_cparams(("arbitrary", "arbitrary")), name="conv_module",
    )(u, halo_src, buf, w, row(conv_b), row(ln_g), row(ln_b))


def _outproj_kernel(*refs, n_lhs):
    lhs, ws = refs[:n_lhs], refs[n_lhs:2 * n_lhs]
    x_ref, g2_ref, gn_ref, sh_ref, sc_ref, wr_ref, br_ref, xn_ref, h_ref, idx_ref, gate_ref = refs[2 * n_lhs:]
    mix = _dot(lhs[0][...], ws[0][...])
    for a, w in zip(lhs[1:], ws[1:]):
        mix = mix + _dot(a[...], w[...])
    xn = x_ref[...] + g2_ref[...] * mix
    xn_ref[...] = xn
    h = _rms_mod(xn, gn_ref[...], sh_ref[...], sc_ref[...])
    h_ref[...] = h
    s = jax.nn.sigmoid(_dot(h.astype(BF16), wr_ref[...]))
    tr = s.shape[0]
    lane = lax.broadcasted_iota(I32, (tr, LANE), 1)
    lanef = lane.astype(F32)
    vals = jnp.where(lane < N_EXPERTS, s + br_ref[...], -jnp.inf)
    idxm = jnp.zeros((tr, LANE), F32)
    gm = jnp.zeros((tr, LANE), F32)
    for j in range(TOP_K):
        m = jnp.max(vals, axis=1, keepdims=True)
        ix = jnp.min(jnp.where(vals == m, lanef, float(LANE)), axis=1, keepdims=True)
        hit = lanef == ix
        sj = jnp.sum(jnp.where(hit, s, 0.0), axis=1, keepdims=True)
        idxm = jnp.where(lane == j, ix, idxm)
        gm = jnp.where(lane == j, sj, gm)
        vals = jnp.where(hit, -jnp.inf, vals)
    idx_ref[...] = idxm.astype(I32)
    gate_ref[...] = gm / jnp.sum(gm, axis=1, keepdims=True) * ROUTED_SCALE


def _outproj_router(lhs, ws, x, gate2, gn, shift, scale, w_router, b_router, group_rows):
    r, d = x.shape
    tr = _row_tile(min(r, group_rows), 512)
    tpg = max(group_rows // tr, 1)
    const = lambda a: pl.BlockSpec(a.shape, lambda i: (0,) * a.ndim)
    rowblk = lambda w: pl.BlockSpec((tr, w), lambda i: (i, 0))
    wr = jnp.pad(w_router, ((0, 0), (0, LANE - N_EXPERTS))).astype(BF16)
    br = jnp.pad(b_router, (0, LANE - N_EXPERTS)).reshape(1, LANE)
    return pl.pallas_call(
        functools.partial(_outproj_kernel, n_lhs=len(lhs)), grid=(r // tr,),
        in_specs=[rowblk(a.shape[1]) for a in lhs] + [const(w) for w in ws]
        + [rowblk(d), _mod_spec(gate2, tr, tpg), const(gn), _mod_spec(shift, tr, tpg), _mod_spec(scale, tr, tpg),
           const(wr), const(br)],
        out_specs=[rowblk(d), rowblk(d), rowblk(LANE), rowblk(LANE)],
        out_shape=[jax.ShapeDtypeStruct((r, d), F32), jax.ShapeDtypeStruct((r, d), F32),
                   jax.ShapeDtypeStruct((r, LANE), I32), jax.ShapeDtypeStruct((r, LANE), F32)],
        compiler_params=_cparams(("arbitrary",)), name="outproj_router",
    )(*lhs, *ws, x, gate2, gn, shift, scale, wr, br)


def _moe_kernel(cnt_ref, off_ref, tok_ref, gate_ref, h_ref, x_ref, g5_ref, wg_ref, wu_ref, wd_ref,
                wgs_ref, wus_ref, wds_ref, o_ref, acc_ref, xg_ref, y_ref, *, t_rows, ch):
    t = pl.program_id(0)
    e = pl.program_id(1)
    ne = pl.num_programs(1)

    def ffn(xb, wg, wu, wd):
        a = _silu(_dot(xb, wg)) * _dot(xb, wu)
        return _dot(a.astype(BF16), wd)

    @pl.when(e == 0)
    def _():
        xg_ref[...] = jnp.zeros(xg_ref.shape, F32)

        def shared(c, _):
            rows = pl.ds(pl.multiple_of(c * ch, ch), ch)
            acc_ref[rows, :] = ffn(h_ref[rows, :].astype(BF16), wgs_ref[...], wus_ref[...], wds_ref[...])
            return 0

        lax.fori_loop(0, t_rows // ch, shared, 0)

    n = cnt_ref[t * ne + e]
    base = off_ref[t * ne + e]

    def chunk(c, _):
        r0 = base + c * ch
        rows = jnp.minimum(ch, n - c * ch)

        def gather(r, _):
            xg_ref[pl.ds(r, 1), :] = h_ref[pl.ds(tok_ref[r0 + r], 1), :]
            return 0

        lax.fori_loop(0, rows, gather, 0)
        y_ref[...] = ffn(xg_ref[...].astype(BF16), wg_ref[...], wu_ref[...], wd_ref[...])

        def scatter(r, _):
            dst = pl.ds(tok_ref[r0 + r], 1)
            acc_ref[dst, :] = acc_ref[dst, :] + gate_ref[r0 + r] * y_ref[pl.ds(r, 1), :]
            return 0

        lax.fori_loop(0, rows, scatter, 0)
        return 0

    lax.fori_loop(0, (n + ch - 1) // ch, chunk, 0)

    @pl.when(e == ne - 1)
    def _():
        o_ref[...] = x_ref[...] + g5_ref[...] * acc_ref[...]


def _moe(h, x, gate5, idx6, g6, wg, wu, wd, wgs, wus, wds, group_rows):
    r, d = h.shape
    t_rows = _row_tile(min(r, group_rows), 1024)
    ch = min(128, t_rows)
    nt = r // t_rows
    per = t_rows * TOP_K
    tpg = max(group_rows // t_rows, 1)
    e = idx6.reshape(nt, per)
    order = jnp.argsort(e, axis=-1, stable=True)
    tok = (order // TOP_K).astype(I32).reshape(-1)
    gs = jnp.take_along_axis(g6.reshape(nt, per), order, axis=-1).reshape(-1)
    cnt = jnp.sum((e[:, :, None] == jnp.arange(N_EXPERTS, dtype=I32)).astype(I32), axis=1)
    off = (jnp.cumsum(cnt, axis=-1) - cnt).astype(I32).reshape(-1)
    cnt = cnt.reshape(-1)
    if gate5.ndim == 3:
        g5spec = pl.BlockSpec((None, 1, d), lambda ti, ei, *_: (ti // tpg, 0, 0))
    else:
        g5spec = pl.BlockSpec((t_rows, d), lambda ti, ei, *_: (ti, 0))
    tile = pl.BlockSpec((t_rows, d), lambda ti, ei, *_: (ti, 0))
    smem = pl.BlockSpec((per,), lambda ti, ei, *_: (ti,), memory_space=pltpu.SMEM)
    expert = lambda a: pl.BlockSpec((None,) + a.shape[1:], lambda ti, ei, *_: (ei, 0, 0))
    const = lambda a: pl.BlockSpec(a.shape, lambda ti, ei, *_: (0,) * a.ndim)
    return pl.pallas_call(
        functools.partial(_moe_kernel, t_rows=t_rows, ch=ch),
        grid_spec=pltpu.PrefetchScalarGridSpec(
            num_scalar_prefetch=2, grid=(nt, N_EXPERTS),
            in_specs=[smem, smem, tile, tile, g5spec, expert(wg), expert(wu), expert(wd),
                      const(wgs), const(wus), const(wds)],
            out_specs=tile,
            scratch_shapes=[pltpu.VMEM((t_rows, d), F32), pltpu.VMEM((ch, d), F32), pltpu.VMEM((ch, d), F32)]),
        out_shape=jax.ShapeDtypeStruct((r, d), F32),
        compiler_params=_cparams(("arbitrary", "arbitrary")), name="moe",
    )(cnt, off, tok, gs, h, x, gate5, wg, wu, wd, wgs, wus, wds)


def _dil_prompt_kernel(q_ref, k_ref, v_ref, tbl_ref, o_ref, *, tq, nd):
    i = pl.program_id(2)
    lo_half = lax.broadcasted_iota(I32, (tq, LANE), 1) < HEAD_DIM
    qp = q_ref[...]
    zero = jnp.zeros((tq, LANE), BF16)
    outs = []
    for sub in range(2):
        qh = jnp.where(lo_half if sub == 0 else ~lo_half, qp, zero)

        def body(j, carry, qh=qh):
            m, l, acc = carry
            rows = pl.ds(pl.multiple_of(j * tq, tq), tq)
            s = _dot_nt(qh, k_ref[rows, :]) * (HEAD_DIM ** -0.5) + tbl_ref[i - j]
            mn = jnp.maximum(m, jnp.max(s, axis=1, keepdims=True))
            p = jnp.exp(s - mn)
            a = jnp.exp(m - mn)
            l = a * l + jnp.sum(p, axis=1, keepdims=True)
            acc = a * acc + _dot(p.astype(BF16), v_ref[rows, :])
            return mn, l, acc

        init = (jnp.full((tq, 1), NEG, F32), jnp.zeros((tq, 1), F32), jnp.zeros((tq, LANE), F32))
        _, l, acc = lax.fori_loop(jnp.maximum(i - (nd - 1), 0), i + 1, body, init)
        outs.append(acc / l)
    o_ref[...] = jnp.where(lo_half, outs[0], outs[1]).astype(o_ref.dtype)


def _dilated_prompt(q, k, v):
    b, s, w = q.shape
    tq = min(256, s)
    nd = max(wd for wd, _ in DILATED_PATTERNS) // tq + 1
    dd = np.arange(nd)[:, None, None]
    rel = dd * tq + np.arange(tq)[None, :, None] - np.arange(tq)[None, None, :]
    tbl = jnp.asarray(_mult_bias(rel))
    return pl.pallas_call(
        functools.partial(_dil_prompt_kernel, tq=tq, nd=nd), grid=(b, w // LANE, s // tq),
        in_specs=[pl.BlockSpec((None, tq, LANE), lambda bi, hp, i: (bi, i, hp)),
                  pl.BlockSpec((None, s, LANE), lambda bi, hp, i: (bi, 0, hp)),
                  pl.BlockSpec((None, s, LANE), lambda bi, hp, i: (bi, 0, hp)),
                  pl.BlockSpec((nd, tq, tq), lambda bi, hp, i: (0, 0, 0))],
        out_specs=pl.BlockSpec((None, tq, LANE), lambda bi, hp, i: (bi, i, hp)),
        out_shape=jax.ShapeDtypeStruct((b, s, w), BF16),
        compiler_params=_cparams(("arbitrary", "arbitrary", "arbitrary")), name="dilated_prompt",
    )(q, k, v, tbl)


def _dil_sample_kernel(qbd_ref, kc_ref, kn_ref, vc_ref, vn_ref, tc_ref, tn_ref, o_ref):
    qbd = qbd_ref[...]
    scale = HEAD_DIM ** -0.5
    sc = _dot(kc_ref[...].astype(BF16), qbd) * scale + tc_ref[...]
    sn = _dot(kn_ref[...].astype(BF16), qbd) * scale + tn_ref[...]
    m = jnp.maximum(jnp.max(sc, axis=0, keepdims=True), jnp.max(sn, axis=0, keepdims=True))
    pc = jnp.exp(sc - m)
    pn = jnp.exp(sn - m)
    l = jnp.sum(pc, axis=0, keepdims=True) + jnp.sum(pn, axis=0, keepdims=True)
    acc = _dot_tn(vc_ref[...].astype(BF16), pc.astype(BF16)) + _dot_tn(vn_ref[...].astype(BF16), pn.astype(BF16))
    o_ref[...] = acc / l


def _dilated_sample(q, k_new, v_new, cache_k, cache_v):
    db, ds, w = q.shape
    wb = cache_k.shape[1]
    cols = N_HEADS_C * ds
    qt = q.reshape(db, ds, N_HEADS_C, HEAD_DIM).transpose(0, 2, 3, 1)
    eye = jnp.eye(N_HEADS_C, dtype=q.dtype)
    qbd = (qt[:, :, :, None, :] * eye[None, :, None, :, None]).reshape(db, w, cols)
    rel = (wb + np.arange(ds)[None, :]) - np.arange(wb + ds)[:, None]
    tbl = np.tile(_mult_bias(rel), (1, N_HEADS_C))
    per_b = lambda *shape: pl.BlockSpec((None,) + shape, lambda b: (b,) + (0,) * len(shape))
    const = lambda a: pl.BlockSpec(a.shape, lambda b: (0,) * a.ndim)
    tc, tn = jnp.asarray(tbl[:wb]), jnp.asarray(tbl[wb:])
    acc_t = pl.pallas_call(
        _dil_sample_kernel, grid=(db,),
        in_specs=[per_b(w, cols), per_b(wb, w), per_b(ds, w), per_b(wb, w), per_b(ds, w), const(tc), const(tn)],
        out_specs=per_b(w, cols),
        out_shape=jax.ShapeDtypeStruct((db, w, cols), F32),
        compiler_params=_cparams(("arbitrary",)), name="dilated_sample",
    )(qbd, cache_k, k_new, cache_v, v_new, tc, tn)
    a5 = acc_t.reshape(db, N_HEADS_C, HEAD_DIM, N_HEADS_C, ds)
    diag = jnp.diagonal(a5, axis1=1, axis2=3)
    return diag.transpose(0, 2, 3, 1).reshape(db, ds, w)


def kernel(x_prompt, x_sample, c_prompt, c_sample, cache_a_k, cache_a_v, cache_a_idx_k, state_conv, cache_c_k, cache_c_v, page_table, norm_mix_g, norm_ffn_g, w_ada, b_ada, w_in_even, w_out_even, a_q_norm_g, a_k_norm_g, b_conv_w, b_conv_b, b_ln_g, b_ln_b, w_in_odd, w_out_odd, c_q_norm_g, c_k_norm_g, w_router, b_router, w_gate_e, w_up_e, w_down_e, w_gate_s, w_up_s, w_down_s):
    bp, sp, d = x_prompt.shape
    bs, ss, _ = x_sample.shape
    depth = w_ada.shape[0]
    past = page_table.shape[1] * cache_a_k.shape[2]
    win_buf = cache_c_k.shape[2]
    rp, rs = bp * sp, bs * ss

    tab_p = _rope_tables(jnp.arange(sp, dtype=I32))
    tab_s = _rope_tables(jnp.tile(past + jnp.arange(ss, dtype=I32), bs))
    row = lambda a: a.reshape(1, -1)
    gain2 = lambda g: jnp.tile(g, 2).reshape(1, LANE)

    xp = x_prompt.reshape(rp, d)
    xs = x_sample.reshape(rs, d)
    c_all = jnp.concatenate([c_prompt, c_sample], axis=0)
    c_all = jnp.pad(c_all, ((0, -c_all.shape[0] % 8), (0, 0)))

    ak_p, av_p, ai_p, ak_s, av_s, ai_s, cv_p, cv_s = [], [], [], [], [], [], [], []
    ck_p, cw_p, ck_s, cw_s = [], [], [], []
    for layer in range(depth):
        mod = _adaln(c_all, w_ada[layer], b_ada[layer])
        mod_p = [m.reshape(bp, 1, d) for m in jnp.split(mod[:bp], 6, axis=-1)]
        mod_s = [jnp.repeat(m, ss, axis=0) for m in jnp.split(mod[bp:bp + bs], 6, axis=-1)]
        gmix, gffn = row(norm_mix_g[layer]), row(norm_ffn_g[layer])
        if layer % 2 == 0:
            e = layer // 2
            w_in = _perm_w_even(w_in_even[e])
            w_out = w_out_even[e].astype(BF16)
            qg, kg = gain2(a_q_norm_g[e]), gain2(a_k_norm_g[e])
            outs = [(A_W, BF16), (A_W, F32), (A_W, BF16), (A_W, F32), (A_W, BF16), (2 * LANE, BF16),
                    (LANE, F32), (LANE, BF16), (LANE, F32), (CONV_CH, F32)]
            q, k, kb, v, vb, qi, ki2, ki2b, wi, u = _project(
                _proj_even_kernel, xp, gmix, mod_p[0], mod_p[1], w_in, tab_p, qg, kg, sp, outs)
            b3 = lambda a: a.reshape(bp, sp, a.shape[-1])
            oa = _dsa_prompt(b3(q), b3(qi), b3(wi), b3(ki2b), b3(kb), b3(vb)).reshape(rp, A_W)
            u3 = b3(u)
            ob = _conv_module(u3, jnp.zeros((bp, CONV_WIDTH - 1, CONV_CH), F32),
                              b_conv_w[e], b_conv_b[e], b_ln_g[e], b_ln_b[e]).reshape(rp, CONV_CH)
            lhs_p = [oa, ob]
            ak_p.append(k.reshape(bp, sp, N_HEADS_A, HEAD_DIM))
            av_p.append(v.reshape(bp, sp, N_HEADS_A, HEAD_DIM))
            ai_p.append(ki2[:, :IDX_DIM].reshape(bp, sp, IDX_DIM))
            xpad = jnp.concatenate([jnp.zeros((bp, CONV_WIDTH - 1, CONV_CH), F32), u3], axis=1)
            cv_p.append(xpad[:, -(CONV_WIDTH - 1):])
            q, k, kb, v, vb, qi, ki2, ki2b, wi, u = _project(
                _proj_even_kernel, xs, gmix, mod_s[0], mod_s[1], w_in, tab_s, qg, kg, rs, outs)
            s3 = lambda a: a.reshape(bs, ss, a.shape[-1])
            oa = _dsa_sample(s3(q), s3(k), s3(v), s3(qi), s3(ki2)[:, :, :IDX_DIM], s3(wi)[:, :, :N_IDX_HEADS],
                             cache_a_k[e], cache_a_v[e], cache_a_idx_k[e], page_table)
            u3 = s3(u)
            ob = _conv_module(u3, state_conv[e], b_conv_w[e], b_conv_b[e], b_ln_g[e], b_ln_b[e])
            lhs_s = [oa.reshape(rs, A_W).astype(BF16), ob.reshape(rs, CONV_CH)]
            ak_s.append(k.reshape(bs, ss, N_HEADS_A, HEAD_DIM))
            av_s.append(v.reshape(bs, ss, N_HEADS_A, HEAD_DIM))
            ai_s.append(ki2[:, :IDX_DIM].reshape(bs, ss, IDX_DIM))
            cv_s.append(jnp.concatenate([state_conv[e], u3], axis=1)[:, -(CONV_WIDTH - 1):])
            ws = [w_out[:A_W], w_out[A_W:]]
        else:
            o = layer // 2
            w_in = w_in_odd[o].astype(BF16)
            qg, kg = gain2(c_q_norm_g[o]), gain2(c_k_norm_g[o])
            outs = [(C_W, BF16), (C_W, F32), (C_W, BF16), (C_W, F32), (C_W, BF16)]
            q, k, kb, v, vb = _project(_proj_odd_kernel, xp, gmix, mod_p[0], mod_p[1], w_in, tab_p, qg, kg, sp, outs)
            b3 = lambda a: a.reshape(bp, sp, a.shape[-1])
            lhs_p = [_dilated_prompt(b3(q), b3(kb), b3(vb)).reshape(rp, C_W)]
            k4, v4 = k.reshape(bp, sp, N_HEADS_C, HEAD_DIM), v.reshape(bp, sp, N_HEADS_C, HEAD_DIM)
            padt = max(win_buf - sp, 0)
            tail = lambda a: jnp.pad(a, ((0, 0), (padt, 0), (0, 0), (0, 0)))[:, -win_buf:]
            ck_p.append(tail(k4))
            cw_p.append(tail(v4))
            q, k, kb, v, vb = _project(_proj_odd_kernel, xs, gmix, mod_s[0], mod_s[1], w_in, tab_s, qg, kg, rs, outs)
            s3 = lambda a: a.reshape(bs, ss, a.shape[-1])
            att = _dilated_sample(s3(q), s3(k), s3(v), cache_c_k[o].reshape(bs, win_buf, C_W),
                                  cache_c_v[o].reshape(bs, win_buf, C_W))
            lhs_s = [att.reshape(rs, C_W).astype(BF16)]
            k4, v4 = k.reshape(bs, ss, N_HEADS_C, HEAD_DIM), v.reshape(bs, ss, N_HEADS_C, HEAD_DIM)
            ck_s.append(jnp.concatenate([cache_c_k[o], k4], axis=1)[:, -win_buf:])
            cw_s.append(jnp.concatenate([cache_c_v[o], v4], axis=1)[:, -win_buf:])
            ws = [w_out_odd[o].astype(BF16)]

        wg, wu, wd = w_gate_e[layer].astype(BF16), w_up_e[layer].astype(BF16), w_down_e[layer].astype(BF16)
        wgs, wus, wds = w_gate_s[layer].astype(BF16), w_up_s[layer].astype(BF16), w_down_s[layer].astype(BF16)
        xn, h, idx, gate = _outproj_router(lhs_p, ws, xp, mod_p[2], gffn, mod_p[3], mod_p[4],
                                           w_router[layer], b_router[layer], sp)
        xp = _moe(h, xn, mod_p[5], idx[:, :TOP_K], gate[:, :TOP_K], wg, wu, wd, wgs, wus, wds, sp)
        xn, h, idx, gate = _outproj_router(lhs_s, ws, xs, mod_s[2], gffn, mod_s[3], mod_s[4],
                                           w_router[layer], b_router[layer], rs)
        xs = _moe(h, xn, mod_s[5], idx[:, :TOP_K], gate[:, :TOP_K], wg, wu, wd, wgs, wus, wds, rs)

    return (xp.reshape(bp, sp, d), xs.reshape(bs, ss, d),
            jnp.stack(ak_p), jnp.stack(av_p), jnp.stack(ai_p),
            jnp.stack(ak_s), jnp.stack(av_s), jnp.stack(ai_s),
            jnp.stack(cv_p), jnp.stack(cv_s),
            jnp.stack(ck_p), jnp.stack(cw_p), jnp.stack(ck_s), jnp.stack(cw_s))
```

```python
import functools

import numpy as np
import jax
import jax.numpy as jnp
from jax import lax
from jax.experimental import pallas as pl
from jax.experimental.pallas import tpu as pltpu

F32, BF16, I32 = jnp.float32, jnp.bfloat16, jnp.int32

HEAD_DIM = 64
N_HEADS_A = 8
CONV_CH = 512
CONV_WIDTH = 31
N_IDX_HEADS = 4
IDX_DIM = 64
TOPK_MAX = 256
N_HEADS_C = 16
DILATED_PATTERNS = ((128, 1), (512, 4), (2048, 16))
N_EXPERTS = 64
TOP_K = 6
ROUTED_SCALE = 2.5
ROPE_THETA = 10000.0
EPS = 1e-6

LANE = 128
ROW_GROUP = 8
NEG = -1e30
KEY_NEG_INF = -2139095041
VMEM_LIMIT = 60 * 1024 * 1024

A_W = N_HEADS_A * HEAD_DIM
C_W = N_HEADS_C * HEAD_DIM


def _cparams(sem):
    return pltpu.CompilerParams(dimension_semantics=sem, vmem_limit_bytes=VMEM_LIMIT)


def _dot(a, b):
    return jnp.dot(a, b, preferred_element_type=F32)


def _dot_nt(a, b):
    return lax.dot_general(a, b, (((1,), (1,)), ((), ())), preferred_element_type=F32)


def _dot_tn(a, b):
    return lax.dot_general(a, b, (((0,), (0,)), ((), ())), preferred_element_type=F32)


def _silu(x):
    return x * jax.nn.sigmoid(x)


def _rms_mod(x, g, shift, scale):
    ms = jnp.mean(x * x, axis=-1, keepdims=True)
    y = x * lax.rsqrt(ms + EPS) * g
    return y * (1.0 + scale) + shift


def _to_key(x):
    b = lax.bitcast_convert_type(x, I32)
    k = jnp.where(b < 0, b ^ 0x7FFFFFFF, b)
    return jnp.where(k == -1, 0, k)


def _rope_tables(pos):
    half = HEAD_DIM // 2
    inv = ROPE_THETA ** (-jnp.arange(half, dtype=F32) / half)
    ang = pos.astype(F32)[:, None] * inv[None, :]
    c, s = jnp.cos(ang), jnp.sin(ang)
    z = jnp.zeros_like(s)
    cos = jnp.tile(c, (1, 4))
    sin_a = jnp.tile(jnp.concatenate([z, s], axis=1), (1, 2))
    sin_b = jnp.tile(jnp.concatenate([-s, z], axis=1), (1, 2))
    return cos, sin_a, sin_b


def _head_sum_matrix():
    r = np.arange(LANE)
    return jnp.asarray((r[:, None] // HEAD_DIM == r[None, :] // HEAD_DIM).astype(np.float32), BF16)


def _prefix_matrix(n):
    r = np.arange(n)
    return jnp.asarray((r[:, None] <= r[None, :]).astype(np.float32), BF16)


def _mult_bias(rel):
    mult = np.zeros(rel.shape, np.int64)
    for w, d in DILATED_PATTERNS:
        mult += ((rel >= 0) & (rel <= w) & (rel % d == 0))
    return np.where(mult > 0, np.log(np.maximum(mult, 1)), NEG).astype(np.float32)


def _ada_kernel(c_ref, w_ref, b_ref, o_ref):
    a = _silu(c_ref[...]).astype(BF16)
    o_ref[...] = _dot(a, w_ref[...].astype(BF16)) + b_ref[...]


def _adaln(c, w, b):
    m, d = c.shape
    n = w.shape[1]
    tn = 1536
    return pl.pallas_call(
        _ada_kernel, grid=(n // tn,),
        in_specs=[pl.BlockSpec((m, d), lambda j: (0, 0)),
                  pl.BlockSpec((d, tn), lambda j: (0, j)),
                  pl.BlockSpec((1, tn), lambda j: (0, j))],
        out_specs=pl.BlockSpec((m, tn), lambda j: (0, j)),
        out_shape=jax.ShapeDtypeStruct((m, n), F32),
        compiler_params=_cparams(("arbitrary",)), name="adaln")(c, w, b.reshape(1, n))


def _norm_rope(z, bsum, g, cos, sin_a, sin_b, norm):
    if norm:
        sq = z * z
        hi = sq.astype(BF16)
        lo = (sq - hi.astype(F32)).astype(BF16)
        ms = (_dot(hi, bsum) + _dot(lo, bsum)) * (1.0 / HEAD_DIM)
        z = z * lax.rsqrt(ms + EPS) * g
    return z * cos + pltpu.roll(z, 32, 1) * sin_a + pltpu.roll(z, 96, 1) * sin_b


def _proj_even_kernel(x_ref, gn_ref, sh_ref, sc_ref, w_ref, cos_ref, sa_ref, sb_ref, qg_ref, kg_ref, bsum_ref,
                      q_ref, k_ref, kb_ref, v_ref, vb_ref, qi_ref, ki_ref, kib_ref, wi_ref, u_ref, h_scr):
    h_scr[...] = _rms_mod(x_ref[...], gn_ref[...], sh_ref[...], sc_ref[...]).astype(BF16)
    hb = h_scr[...]
    cos, sa, sb, bsum = cos_ref[...], sa_ref[...], sb_ref[...], bsum_ref[...]

    def seg(lo, n):
        return _dot(hb, w_ref[:, lo:lo + n])

    z = seg(0, A_W)
    for c in range(A_W // LANE):
        sl = slice(c * LANE, (c + 1) * LANE)
        q_ref[:, sl] = _norm_rope(z[:, sl], bsum, qg_ref[...], cos, sa, sb, True).astype(BF16)
    z = seg(A_W, A_W)
    for c in range(A_W // LANE):
        sl = slice(c * LANE, (c + 1) * LANE)
        kk = _norm_rope(z[:, sl], bsum, kg_ref[...], cos, sa, sb, True)
        k_ref[:, sl] = kk
        kb_ref[:, sl] = kk.astype(BF16)
    z = seg(2 * A_W, A_W)
    v_ref[...] = z
    vb_ref[...] = z.astype(BF16)
    z = seg(3 * A_W, 2 * LANE)
    for c in range(2):
        sl = slice(c * LANE, (c + 1) * LANE)
        qi_ref[:, sl] = _norm_rope(z[:, sl], bsum, None, cos, sa, sb, False).astype(BF16)
    z = _norm_rope(seg(3 * A_W + 2 * LANE, LANE), bsum, None, cos, sa, sb, False)
    ki_ref[...] = z
    kib_ref[...] = z.astype(BF16)
    off = 3 * A_W + 3 * LANE
    ua = seg(off, CONV_CH)
    ub = seg(off + CONV_CH, CONV_CH)
    u_ref[...] = ua * jax.nn.sigmoid(ub)
    wi_ref[...] = seg(off + 2 * CONV_CH, LANE) * (N_IDX_HEADS ** -0.5 * IDX_DIM ** -0.5)


def _proj_odd_kernel(x_ref, gn_ref, sh_ref, sc_ref, w_ref, cos_ref, sa_ref, sb_ref, qg_ref, kg_ref, bsum_ref,
                     q_ref, k_ref, kb_ref, v_ref, vb_ref, h_scr):
    h_scr[...] = _rms_mod(x_ref[...], gn_ref[...], sh_ref[...], sc_ref[...]).astype(BF16)
    hb = h_scr[...]
    cos, sa, sb, bsum = cos_ref[...], sa_ref[...], sb_ref[...], bsum_ref[...]
    for half in range(2):
        z = _dot(hb, w_ref[:, half * A_W:(half + 1) * A_W])
        for c in range(A_W // LANE):
            sl = slice(c * LANE, (c + 1) * LANE)
            dl = slice(half * A_W + c * LANE, half * A_W + (c + 1) * LANE)
            q_ref[:, dl] = _norm_rope(z[:, sl], bsum, qg_ref[...], cos, sa, sb, True).astype(BF16)
    for half in range(2):
        z = _dot(hb, w_ref[:, C_W + half * A_W:C_W + (half + 1) * A_W])
        for c in range(A_W // LANE):
            sl = slice(c * LANE, (c + 1) * LANE)
            dl = slice(half * A_W + c * LANE, half * A_W + (c + 1) * LANE)
            kk = _norm_rope(z[:, sl], bsum, kg_ref[...], cos, sa, sb, True)
            k_ref[:, dl] = kk
            kb_ref[:, dl] = kk.astype(BF16)
    for half in range(2):
        z = _dot(hb, w_ref[:, 2 * C_W + half * A_W:2 * C_W + (half + 1) * A_W])
        v_ref[:, half * A_W:(half + 1) * A_W] = z
        vb_ref[:, half * A_W:(half + 1) * A_W] = z.astype(BF16)


def _mod_spec(mod, tr, tiles_per_group):
    d = mod.shape[-1]
    if mod.ndim == 3:
        return pl.BlockSpec((None, 1, d), lambda i: (i // tiles_per_group, 0, 0))
    return pl.BlockSpec((tr, d), lambda i: (i, 0))


def _row_tile(rows, want):
    return want if rows % want == 0 else rows


def _project(kern, x, gn, shift, scale, w, tables, qg, kg, group_rows, outs):
    r, d = x.shape
    tr = _row_tile(min(r, group_rows), 512)
    tpg = max(group_rows // tr, 1)
    tpt = tables[0].shape[0] // tr
    const = lambda a: pl.BlockSpec(a.shape, lambda i: (0,) * a.ndim)
    tab = pl.BlockSpec((tr, LANE), lambda i: (i % tpt, 0))
    bsum = _head_sum_matrix()
    return pl.pallas_call(
        kern, grid=(r // tr,),
        in_specs=[pl.BlockSpec((tr, d), lambda i: (i, 0)), const(gn), _mod_spec(shift, tr, tpg),
                  _mod_spec(scale, tr, tpg), const(w), tab, tab, tab, const(qg), const(kg), const(bsum)],
        out_specs=[pl.BlockSpec((tr, wd), lambda i: (i, 0)) for wd, _ in outs],
        out_shape=[jax.ShapeDtypeStruct((r, wd), dt) for wd, dt in outs],
        scratch_shapes=[pltpu.VMEM((tr, d), BF16)],
        compiler_params=_cparams(("arbitrary",)), name=kern.__name__.strip("_"),
    )(x, gn, shift, scale, w, *tables, qg, kg, bsum)


def _perm_w_even(w):
    d = w.shape[0]
    cuts = np.cumsum([A_W, A_W, A_W, N_IDX_HEADS * IDX_DIM, IDX_DIM, N_IDX_HEADS, CONV_CH])
    q, k, v, qi, ki, wi, ua, ub = jnp.split(w, [int(c) for c in cuts], axis=1)
    pad = jnp.zeros((d, LANE - N_IDX_HEADS), w.dtype)
    return jnp.concatenate([q, k, v, qi, ki, ki, ua, ub, wi, pad], axis=1).astype(BF16)


def _dsa_prompt_kernel(q_ref, qi_ref, wi_ref, ki_ref, k_ref, v_ref, u_ref, o_ref, key_ref, bias_ref,
                       qs_ref, m_ref, l_ref, acc_ref, *, tq, kc, topk):
    i = pl.program_id(1)
    q0 = i * tq
    cd = q0 // kc
    lane = lax.broadcasted_iota(I32, (tq, LANE), 1)
    lo_half = lane < HEAD_DIM
    zero = jnp.zeros((tq, LANE), BF16)

    qi = qi_ref[...]
    qm = [jnp.where(lo_half if h % 2 == 0 else ~lo_half, qi[:, (h // 2) * LANE:(h // 2 + 1) * LANE], zero)
          for h in range(N_IDX_HEADS)]
    wi = wi_ref[...]
    wcol = [wi[:, h:h + 1] for h in range(N_IDX_HEADS)]

    def score_chunk(c):
        kk = ki_ref[pl.ds(pl.multiple_of(c * kc, kc), kc), :]
        acc = None
        for h in range(N_IDX_HEADS):
            t = wcol[h] * jnp.maximum(_dot_nt(qm[h], kk), 0.0)
            acc = t if acc is None else acc + t
        return acc

    def fill(c, _):
        key_ref[c] = _to_key(score_chunk(c))
        return 0

    lax.fori_loop(0, cd, fill, 0)
    kpos = cd * kc + lax.broadcasted_iota(I32, (tq, kc), 1)
    qpos = q0 + lax.broadcasted_iota(I32, (tq, kc), 0)
    key_ref[cd] = _to_key(jnp.where(kpos <= qpos, score_chunk(cd), -jnp.inf))

    def count(cand, strict):
        def body(c, cnt):
            kk = key_ref[c]
            one = jnp.where((kk > cand) if strict else (kk >= cand), 1.0, 0.0)
            part = one[:, 0:LANE]
            for j in range(1, kc // LANE):
                part = part + one[:, j * LANE:(j + 1) * LANE]
            return cnt + part
        cnt = lax.fori_loop(0, cd + 1, body, jnp.zeros((tq, LANE), F32))
        return jnp.sum(cnt, axis=1, keepdims=True)

    kf = float(topk)
    thr = jnp.where(count(jnp.zeros((tq, 1), I32), False) >= kf, 0, -2 ** 31).astype(I32)

    def bit(b, thr):
        cand = thr + jnp.left_shift(jnp.int32(1), 30 - b)
        return jnp.where(count(cand, False) >= kf, cand, thr)

    thr = lax.fori_loop(0, 31, bit, thr)
    need = kf - count(thr, True)
    upper = u_ref[...]

    def select(c, run):
        kk = key_ref[c]
        tie = kk == thr
        pre = _dot(jnp.where(tie, 1.0, 0.0).astype(BF16), upper) + run
        sel = ((kk > thr) | (tie & (pre <= need))) & (kk > KEY_NEG_INF)
        bias_ref[c] = jnp.where(sel, 0.0, NEG)
        return pre[:, kc - 1:kc]

    lax.fori_loop(0, cd + 1, select, jnp.zeros((tq, 1), F32))

    n_pairs = A_W // LANE
    for hp in range(n_pairs):
        qs_ref[hp] = _stack_pair(q_ref[:, hp * LANE:(hp + 1) * LANE], lo_half)
    _flash_init(m_ref, l_ref, acc_ref)

    def attend(c, _):
        rows = pl.ds(pl.multiple_of(c * kc, kc), kc)
        b = bias_ref[c]
        bias2 = jnp.concatenate([b, b], axis=0)
        for hp in range(n_pairs):
            cols = slice(hp * LANE, (hp + 1) * LANE)
            _flash_step(qs_ref[hp], k_ref[rows, cols], v_ref[rows, cols], bias2, m_ref, l_ref, acc_ref, hp)
        return 0

    lax.fori_loop(0, cd + 1, attend, 0)
    for hp in range(n_pairs):
        o_ref[:, hp * LANE:(hp + 1) * LANE] = _flash_finish(l_ref, acc_ref, hp, lo_half, tq).astype(o_ref.dtype)


def _stack_pair(qp, lo_half):
    qsc = (qp.astype(F32) * (HEAD_DIM ** -0.5)).astype(BF16)
    zero = jnp.zeros_like(qsc)
    return jnp.concatenate([jnp.where(lo_half, qsc, zero), jnp.where(lo_half, zero, qsc)], axis=0)


def _flash_init(m_ref, l_ref, acc_ref):
    m_ref[...] = jnp.full(m_ref.shape, NEG, F32)
    l_ref[...] = jnp.zeros(l_ref.shape, F32)
    acc_ref[...] = jnp.zeros(acc_ref.shape, F32)


def _flash_step(qs, kblk, vblk, bias2, m_ref, l_ref, acc_ref, idx):
    s = _dot_nt(qs, kblk) + bias2
    m_old = m_ref[idx]
    mn = jnp.maximum(m_old, jnp.max(s, axis=1, keepdims=True))
    p = jnp.exp(s - jnp.tile(mn, (1, s.shape[1] // LANE)))
    a = jnp.exp(m_old - mn)
    l_ref[idx] = a * l_ref[idx] + jnp.sum(p, axis=1, keepdims=True)
    acc_ref[idx] = a * acc_ref[idx] + _dot(p.astype(BF16), vblk)
    m_ref[idx] = mn


def _flash_finish(l_ref, acc_ref, idx, lo_half, t):
    o = acc_ref[idx] / l_ref[idx]
    return jnp.where(lo_half, o[:t], o[t:])


def _dsa_prompt(q, qi, wi, ki2, k, v):
    b, s, _ = q.shape
    tq = 128
    kc = min(512, s)
    topk = min(TOPK_MAX, s // 4)
    nc = s // kc
    upper = _prefix_matrix(kc)
    qspec = lambda w: pl.BlockSpec((None, tq, w), lambda bi, i: (bi, i, 0))
    kspec = lambda w: pl.BlockSpec((None, s, w), lambda bi, i: (bi, 0, 0))
    return pl.pallas_call(
        functools.partial(_dsa_prompt_kernel, tq=tq, kc=kc, topk=topk),
        grid=(b, s // tq),
        in_specs=[qspec(A_W), qspec(2 * LANE), qspec(LANE), kspec(LANE), kspec(A_W), kspec(A_W),
                  pl.BlockSpec((kc, kc), lambda bi, i: (0, 0))],
        out_specs=qspec(A_W),
        out_shape=jax.ShapeDtypeStruct((b, s, A_W), BF16),
        scratch_shapes=[pltpu.VMEM((nc, tq, kc), I32), pltpu.VMEM((nc, tq, kc), F32),
                        pltpu.VMEM((A_W // LANE, 2 * tq, LANE), BF16)]
        + [pltpu.VMEM((A_W // LANE, 2 * tq, LANE), F32)] * 3,
        compiler_params=_cparams(("arbitrary", "arbitrary")), name="dsa_prompt",
    )(q, qi, wi, ki2, k, v, upper)


def _dsa_sample_select_kernel(pt_ref, qi_ref, wi_ref, *rest, n_new, npg, pg, topk):
    del pt_ref
    kip_refs = rest[:pg]
    kin_ref, u_ref, o_ref, key_ref = rest[pg:]
    p = pl.program_id(1)
    qi = qi_ref[...]
    wi = wi_ref[...]
    n_chunks = key_ref.shape[0]

    def score(ki):
        r = jnp.maximum(_dot_nt(qi, ki.astype(BF16)), 0.0) * wi
        acc = r[0:n_new]
        for h in range(1, N_IDX_HEADS):
            acc = acc + r[h * n_new:(h + 1) * n_new]
        return acc

    for g in range(pg):
        key_ref[p * pg + g] = _to_key(score(kip_refs[g][...]))

    @pl.when(p == 0)
    def _():
        for c in range(npg + 1, n_chunks):
            key_ref[c] = jnp.full((n_new, LANE), -2 ** 31, I32)

    @pl.when(p == npg // pg - 1)
    def _():
        col = lax.broadcasted_iota(I32, (n_new, LANE), 1)
        row = lax.broadcasted_iota(I32, (n_new, LANE), 0)
        key_ref[npg] = _to_key(jnp.where(col <= row, score(kin_ref[...]), -jnp.inf))

        def count(cand, strict):
            def body(c, cnt):
                kk = key_ref[pl.ds(pl.multiple_of(c * ROW_GROUP, ROW_GROUP), ROW_GROUP)]
                hit = jnp.where((kk > cand) if strict else (kk >= cand), 1.0, 0.0)
                return cnt + jnp.sum(hit, axis=0)
            cnt = lax.fori_loop(0, n_chunks // ROW_GROUP, body, jnp.zeros((n_new, LANE), F32))
            return jnp.sum(cnt, axis=1, keepdims=True)

        kf = float(topk)
        thr = jnp.where(count(jnp.zeros((n_new, 1), I32), False) >= kf, 0, -2 ** 31).astype(I32)

        def bit(b, thr):
            cand = thr + jnp.left_shift(jnp.int32(1), 30 - b)
            return jnp.where(count(cand, False) >= kf, cand, thr)

        thr = lax.fori_loop(0, 31, bit, thr)
        need = kf - count(thr, True)
        upper = u_ref[...]

        def select(c, run):
            kk = key_ref[c]
            tie = kk == thr
            pre = _dot(jnp.where(tie, 1.0, 0.0).astype(BF16), upper) + run
            sel = ((kk > thr) | (tie & (pre <= need))) & (kk > KEY_NEG_INF)
            o_ref[c] = jnp.where(sel, 1.0, 0.0)
            return pre[:, LANE - 1:LANE]

        lax.fori_loop(0, npg + 1, select, jnp.zeros((n_new, 1), F32))


def _dsa_sample_attend_kernel(pt_ref, qbd_ref, *rest, npg, pg):
    del pt_ref
    kp_refs, vp_refs = rest[:pg], rest[pg:2 * pg]
    kn_ref, vn_ref, sel_ref, rep_ref, o_ref, m_ref, l_ref, acc_ref = rest[2 * pg:]
    p = pl.program_id(1)

    @pl.when(p == 0)
    def _():
        _flash_init(m_ref, l_ref, acc_ref)

    def update(blocks):
        scores = []
        for kk, _, sel in blocks:
            keep = _dot_tn(sel.astype(BF16), rep_ref[...])
            scores.append(_dot(kk.astype(BF16), qbd_ref[...]) * (HEAD_DIM ** -0.5) + (keep - 1.0) * (-NEG))
        m = m_ref[...]
        mn = m
        for s in scores:
            mn = jnp.maximum(mn, jnp.max(s, axis=0, keepdims=True))
        a = jnp.exp(m - mn)
        l = a * l_ref[...]
        acc = a * acc_ref[...]
        for s, (_, vv, _) in zip(scores, blocks):
            pr = jnp.exp(s - mn)
            l = l + jnp.sum(pr, axis=0, keepdims=True)
            acc = acc + _dot_tn(vv.astype(BF16), pr.astype(BF16))
        m_ref[...] = mn
        l_ref[...] = l
        acc_ref[...] = acc

    update([(kp_refs[g][...], vp_refs[g][...], sel_ref[p * pg + g]) for g in range(pg)])

    @pl.when(p == npg // pg - 1)
    def _():
        update([(kn_ref[...], vn_ref[...], sel_ref[npg])])
        o_ref[...] = acc_ref[...] / l_ref[...]


def _dsa_sample(q, k_new, v_new, qi, ki_new, wi, pool_k, pool_v, pool_ki, page_table):
    db, ds, _ = q.shape
    n_phys, page = pool_k.shape[0], pool_k.shape[1]
    npg = page_table.shape[1]
    topk = min(TOPK_MAX, (npg * page + ds) // 4)
    pt = page_table.reshape(-1).astype(I32)
    hq = N_IDX_HEADS * ds
    qi_r = qi.reshape(db, ds, N_IDX_HEADS, IDX_DIM).transpose(0, 2, 1, 3).reshape(db, hq, IDX_DIM)
    wi_r = jnp.broadcast_to(wi.transpose(0, 2, 1).reshape(db, hq, 1), (db, hq, LANE))
    pad = lambda a: jnp.pad(a, ((0, 0), (0, page - ds), (0, 0)))
    per_b = lambda *shape: pl.BlockSpec((None,) + shape, lambda b, p, pt: (b,) + (0,) * len(shape))
    pg = next(g for g in (8, 4, 2, 1) if npg % g == 0)
    paged = lambda w: [pl.BlockSpec((None, page, w), lambda b, p, pt, g=g: (pt[b * npg + p * pg + g], 0, 0))
                       for g in range(pg)]
    n_chunks = -(-(npg + 1) // ROW_GROUP) * ROW_GROUP
    sel = pl.pallas_call(
        functools.partial(_dsa_sample_select_kernel, n_new=ds, npg=npg, pg=pg, topk=topk),
        grid_spec=pltpu.PrefetchScalarGridSpec(
            num_scalar_prefetch=1, grid=(db, npg // pg),
            in_specs=[per_b(hq, IDX_DIM), per_b(hq, LANE)] + paged(IDX_DIM)
            + [per_b(page, IDX_DIM), pl.BlockSpec((page, page), lambda b, p, pt: (0, 0))],
            out_specs=per_b(npg + 1, ds, page),
            scratch_shapes=[pltpu.VMEM((n_chunks, ds, page), I32)]),
        out_shape=jax.ShapeDtypeStruct((db, npg + 1, ds, page), F32),
        compiler_params=_cparams(("arbitrary", "arbitrary")), name="dsa_sample_select",
    )(pt, qi_r, wi_r, *([pool_ki] * pg), pad(ki_new), _prefix_matrix(page))

    cols = N_HEADS_A * ds
    qt = q.reshape(db, ds, N_HEADS_A, HEAD_DIM).transpose(0, 2, 3, 1)
    eye = jnp.eye(N_HEADS_A, dtype=q.dtype)
    qbd = (qt[:, :, :, None, :] * eye[None, :, None, :, None]).reshape(db, A_W, cols)
    rep = jnp.tile(jnp.eye(ds, dtype=BF16), (1, N_HEADS_A))
    pool_k2, pool_v2 = pool_k.reshape(n_phys, page, A_W), pool_v.reshape(n_phys, page, A_W)
    acc_t = pl.pallas_call(
        functools.partial(_dsa_sample_attend_kernel, npg=npg, pg=pg),
        grid_spec=pltpu.PrefetchScalarGridSpec(
            num_scalar_prefetch=1, grid=(db, npg // pg),
            in_specs=[per_b(A_W, cols)] + paged(A_W) + paged(A_W)
            + [per_b(page, A_W), per_b(page, A_W), per_b(npg + 1, ds, page),
               pl.BlockSpec((ds, cols), lambda b, p, pt: (0, 0))],
            out_specs=per_b(A_W, cols),
            scratch_shapes=[pltpu.VMEM((1, cols), F32), pltpu.VMEM((1, cols), F32), pltpu.VMEM((A_W, cols), F32)]),
        out_shape=jax.ShapeDtypeStruct((db, A_W, cols), F32),
        compiler_params=_cparams(("arbitrary", "arbitrary")), name="dsa_sample_attend",
    )(pt, qbd, *([pool_k2] * pg), *([pool_v2] * pg), pad(k_new), pad(v_new), sel, rep)
    a5 = acc_t.reshape(db, N_HEADS_A, HEAD_DIM, N_HEADS_A, ds)
    diag = jnp.diagonal(a5, axis1=1, axis2=3)
    return diag.transpose(0, 2, 3, 1).reshape(db, ds, A_W)


def _conv_kernel(u_ref, halo_ref, buf_ref, w_ref, b_ref, g_ref, bb_ref, o_ref, xp_ref, *, tt):
    i = pl.program_id(1)

    @pl.when(i == 0)
    def _():
        xp_ref[0:32, :] = buf_ref[...]

    @pl.when(i > 0)
    def _():
        xp_ref[0:32, :] = halo_ref[...]

    xp_ref[32:32 + tt, :] = u_ref[...]
    w = w_ref[...]
    acc = jnp.zeros((tt, CONV_CH), F32) + b_ref[...]
    for j in range(CONV_WIDTH):
        acc = acc + xp_ref[2 + j:2 + j + tt, :] * w[j:j + 1, :]
    mu = jnp.mean(acc, axis=-1, keepdims=True)
    var = jnp.mean(jnp.square(acc - mu), axis=-1, keepdims=True)
    yn = (acc - mu) * lax.rsqrt(var + EPS) * g_ref[...] + bb_ref[...]
    o_ref[...] = _silu(yn).astype(o_ref.dtype)


def _conv_module(u, buf30, conv_w, conv_b, ln_g, ln_b):
    b, t, c = u.shape
    tt = _row_tile(t, 512)
    buf = jnp.pad(buf30, ((0, 0), (2, 0), (0, 0)))
    halo_src = u if t >= 32 else buf
    hb = tt // 32
    w = jnp.pad(conv_w, ((0, 32 - CONV_WIDTH), (0, 0)))
    const = lambda a: pl.BlockSpec(a.shape, lambda bi, i: (0,) * a.ndim)
    row = lambda a: a.reshape(1, c)
    return pl.pallas_call(
        functools.partial(_conv_kernel, tt=tt), grid=(b, t // tt),
        in_specs=[pl.BlockSpec((None, tt, c), lambda bi, i: (bi, i, 0)),
                  pl.BlockSpec((None, 32, c), lambda bi, i: (bi, jnp.maximum(i * hb - 1, 0), 0)),
                  pl.BlockSpec((None, 32, c), lambda bi, i: (bi, 0, 0)),
                  const(w), const(row(conv_b)), const(row(ln_g)), const(row(ln_b))],
        out_specs=pl.BlockSpec((None, tt, c), lambda bi, i: (bi, i, 0)),
        out_shape=jax.ShapeDtypeStruct((b, t, c), BF16),
        scratch_shapes=[pltpu.VMEM((32 + tt, c), F32)],
        compiler_params=_cparams(("arbitrary", "arbitrary")), name="conv_module",
    )(u, halo_src, buf, w, row(conv_b), row(ln_g), row(ln_b))


def _outproj_kernel(*refs, n_lhs):
    lhs, ws = refs[:n_lhs], refs[n_lhs:2 * n_lhs]
    x_ref, g2_ref, gn_ref, sh_ref, sc_ref, wr_ref, br_ref, xn_ref, h_ref, idx_ref, gate_ref = refs[2 * n_lhs:]
    mix = _dot(lhs[0][...], ws[0][...])
    for a, w in zip(lhs[1:], ws[1:]):
        mix = mix + _dot(a[...], w[...])
    xn = x_ref[...] + g2_ref[...] * mix
    xn_ref[...] = xn
    h = _rms_mod(xn, gn_ref[...], sh_ref[...], sc_ref[...])
    h_ref[...] = h
    s = jax.nn.sigmoid(_dot(h.astype(BF16), wr_ref[...]))
    tr = s.shape[0]
    lane = lax.broadcasted_iota(I32, (tr, LANE), 1)
    lanef = lane.astype(F32)
    vals = jnp.where(lane < N_EXPERTS, s + br_ref[...], -jnp.inf)
    idxm = jnp.zeros((tr, LANE), F32)
    gm = jnp.zeros((tr, LANE), F32)
    for j in range(TOP_K):
        m = jnp.max(vals, axis=1, keepdims=True)
        ix = jnp.min(jnp.where(vals == m, lanef, float(LANE)), axis=1, keepdims=True)
        hit = lanef == ix
        sj = jnp.sum(jnp.where(hit, s, 0.0), axis=1, keepdims=True)
        idxm = jnp.where(lane == j, ix, idxm)
        gm = jnp.where(lane == j, sj, gm)
        vals = jnp.where(hit, -jnp.inf, vals)
    idx_ref[...] = idxm.astype(I32)
    gate_ref[...] = gm / jnp.sum(gm, axis=1, keepdims=True) * ROUTED_SCALE


def _outproj_router(lhs, ws, x, gate2, gn, shift, scale, w_router, b_router, group_rows):
    r, d = x.shape
    tr = _row_tile(min(r, group_rows), 512)
    tpg = max(group_rows // tr, 1)
    const = lambda a: pl.BlockSpec(a.shape, lambda i: (0,) * a.ndim)
    rowblk = lambda w: pl.BlockSpec((tr, w), lambda i: (i, 0))
    wr = jnp.pad(w_router, ((0, 0), (0, LANE - N_EXPERTS))).astype(BF16)
    br = jnp.pad(b_router, (0, LANE - N_EXPERTS)).reshape(1, LANE)
    return pl.pallas_call(
        functools.partial(_outproj_kernel, n_lhs=len(lhs)), grid=(r // tr,),
        in_specs=[rowblk(a.shape[1]) for a in lhs] + [const(w) for w in ws]
        + [rowblk(d), _mod_spec(gate2, tr, tpg), const(gn), _mod_spec(shift, tr, tpg), _mod_spec(scale, tr, tpg),
           const(wr), const(br)],
        out_specs=[rowblk(d), rowblk(d), rowblk(LANE), rowblk(LANE)],
        out_shape=[jax.ShapeDtypeStruct((r, d), F32), jax.ShapeDtypeStruct((r, d), F32),
                   jax.ShapeDtypeStruct((r, LANE), I32), jax.ShapeDtypeStruct((r, LANE), F32)],
        compiler_params=_cparams(("arbitrary",)), name="outproj_router",
    )(*lhs, *ws, x, gate2, gn, shift, scale, wr, br)


def _moe_kernel(cnt_ref, off_ref, tok_ref, gate_ref, h_ref, x_ref, g5_ref, wg_ref, wu_ref, wd_ref,
                wgs_ref, wus_ref, wds_ref, o_ref, acc_ref, xg_ref, y_ref, *, t_rows, ch):
    t = pl.program_id(0)
    e = pl.program_id(1)
    ne = pl.num_programs(1)

    def ffn(xb, wg, wu, wd):
        a = _silu(_dot(xb, wg)) * _dot(xb, wu)
        return _dot(a.astype(BF16), wd)

    @pl.when(e == 0)
    def _():
        xg_ref[...] = jnp.zeros(xg_ref.shape, F32)
        acc_ref[t_rows:t_rows + ROW_GROUP, :] = jnp.zeros((ROW_GROUP, acc_ref.shape[1]), F32)

        def shared(c, _):
            rows = pl.ds(pl.multiple_of(c * ch, ch), ch)
            acc_ref[rows, :] = ffn(h_ref[rows, :].astype(BF16), wgs_ref[...], wus_ref[...], wds_ref[...])
            return 0

        lax.fori_loop(0, t_rows // ch, shared, 0)

    n = cnt_ref[t * ne + e]
    base = off_ref[t * ne + e]

    def chunk(c, _):
        r0 = base + c * ch
        rows = jnp.minimum(ch, n - c * ch)
        groups = lax.shift_right_logical(rows + (ROW_GROUP - 1), ROW_GROUP.bit_length() - 1)

        def gather(g, _):
            r = r0 + g * ROW_GROUP
            picked = [h_ref[pl.ds(tok_ref[r + j], 1), :] for j in range(ROW_GROUP)]
            xg_ref[pl.ds(pl.multiple_of(g * ROW_GROUP, ROW_GROUP), ROW_GROUP), :] = jnp.concatenate(picked, axis=0)
            return 0

        lax.fori_loop(0, groups, gather, 0)
        y_ref[...] = ffn(xg_ref[...].astype(BF16), wg_ref[...], wu_ref[...], wd_ref[...])

        def scatter(g, _):
            y8 = y_ref[pl.ds(pl.multiple_of(g * ROW_GROUP, ROW_GROUP), ROW_GROUP), :]
            dst, gate = [], []
            for j in range(ROW_GROUP):
                r = g * ROW_GROUP + j
                ok = r < rows
                dst.append(jnp.where(ok, tok_ref[r0 + r], t_rows + j))
                gate.append(jnp.where(ok, gate_ref[r0 + r], 0.0))
            new = [acc_ref[pl.ds(dst[j], 1), :] + gate[j] * y8[j:j + 1, :] for j in range(ROW_GROUP)]
            for j in range(ROW_GROUP):
                acc_ref[pl.ds(dst[j], 1), :] = new[j]
            return 0

        lax.fori_loop(0, groups, scatter, 0)
        return 0

    lax.fori_loop(0, (n + ch - 1) // ch, chunk, 0)

    @pl.when(e == ne - 1)
    def _():
        o_ref[...] = x_ref[...] + g5_ref[...] * acc_ref[0:t_rows, :]


def _moe(h, x, gate5, idx6, g6, wg, wu, wd, wgs, wus, wds, group_rows):
    r, d = h.shape
    t_rows = _row_tile(min(r, group_rows), 1024)
    ch = min(128, t_rows)
    nt = r // t_rows
    per = t_rows * TOP_K
    tpg = max(group_rows // t_rows, 1)
    e = idx6.reshape(nt, per)
    order = jnp.argsort(e, axis=-1, stable=True)
    per_pad = -(-(per + ROW_GROUP) // 1024) * 1024
    padl = lambda a: jnp.pad(a, ((0, 0), (0, per_pad - per))).reshape(-1)
    tok = padl((order // TOP_K).astype(I32))
    gs = padl(jnp.take_along_axis(g6.reshape(nt, per), order, axis=-1))
    cnt = jnp.sum((e[:, :, None] == jnp.arange(N_EXPERTS, dtype=I32)).astype(I32), axis=1)
    off = (jnp.cumsum(cnt, axis=-1) - cnt).astype(I32).reshape(-1)
    cnt = cnt.reshape(-1)
    if gate5.ndim == 3:
        g5spec = pl.BlockSpec((None, 1, d), lambda ti, ei, *_: (ti // tpg, 0, 0))
    else:
        g5spec = pl.BlockSpec((t_rows, d), lambda ti, ei, *_: (ti, 0))
    tile = pl.BlockSpec((t_rows, d), lambda ti, ei, *_: (ti, 0))
    smem = pl.BlockSpec((per_pad,), lambda ti, ei, *_: (ti,), memory_space=pltpu.SMEM)
    expert = lambda a: pl.BlockSpec((None,) + a.shape[1:], lambda ti, ei, *_: (ei, 0, 0))
    const = lambda a: pl.BlockSpec(a.shape, lambda ti, ei, *_: (0,) * a.ndim)
    return pl.pallas_call(
        functools.partial(_moe_kernel, t_rows=t_rows, ch=ch),
        grid_spec=pltpu.PrefetchScalarGridSpec(
            num_scalar_prefetch=2, grid=(nt, N_EXPERTS),
            in_specs=[smem, smem, tile, tile, g5spec, expert(wg), expert(wu), expert(wd),
                      const(wgs), const(wus), const(wds)],
            out_specs=tile,
            scratch_shapes=[pltpu.VMEM((t_rows + ROW_GROUP, d), F32), pltpu.VMEM((ch, d), F32),
                            pltpu.VMEM((ch, d), F32)]),
        out_shape=jax.ShapeDtypeStruct((r, d), F32),
        compiler_params=_cparams(("arbitrary", "arbitrary")), name="moe",
    )(cnt, off, tok, gs, h, x, gate5, wg, wu, wd, wgs, wus, wds)


def _dil_prompt_kernel(q_ref, k_ref, v_ref, tbl_ref, o_ref, qs_ref, m_ref, l_ref, acc_ref, *, tq, nd, n_pairs):
    i = pl.program_id(2)
    lo_half = lax.broadcasted_iota(I32, (tq, LANE), 1) < HEAD_DIM
    for hp in range(n_pairs):
        qs_ref[hp] = _stack_pair(q_ref[:, hp * LANE:(hp + 1) * LANE], lo_half)
    _flash_init(m_ref, l_ref, acc_ref)

    def attend(j, _):
        rows = pl.ds(pl.multiple_of(j * tq, tq), tq)
        b = tbl_ref[i - j]
        bias2 = jnp.concatenate([b, b], axis=0)
        for hp in range(n_pairs):
            cols = slice(hp * LANE, (hp + 1) * LANE)
            _flash_step(qs_ref[hp], k_ref[rows, cols], v_ref[rows, cols], bias2, m_ref, l_ref, acc_ref, hp)
        return 0

    lax.fori_loop(jnp.maximum(i - (nd - 1), 0), i + 1, attend, 0)
    for hp in range(n_pairs):
        o_ref[:, hp * LANE:(hp + 1) * LANE] = _flash_finish(l_ref, acc_ref, hp, lo_half, tq).astype(o_ref.dtype)


def _dilated_prompt(q, k, v):
    b, s, w = q.shape
    tq = min(256, s)
    nd = max(wd for wd, _ in DILATED_PATTERNS) // tq + 1
    dd = np.arange(nd)[:, None, None]
    rel = dd * tq + np.arange(tq)[None, :, None] - np.arange(tq)[None, None, :]
    tbl = jnp.asarray(_mult_bias(rel))
    n_pairs = 2
    gw = n_pairs * LANE
    state = pltpu.VMEM((n_pairs, 2 * tq, LANE), F32)
    return pl.pallas_call(
        functools.partial(_dil_prompt_kernel, tq=tq, nd=nd, n_pairs=n_pairs), grid=(b, w // gw, s // tq),
        in_specs=[pl.BlockSpec((None, tq, gw), lambda bi, hp, i: (bi, i, hp)),
                  pl.BlockSpec((None, s, gw), lambda bi, hp, i: (bi, 0, hp)),
                  pl.BlockSpec((None, s, gw), lambda bi, hp, i: (bi, 0, hp)),
                  pl.BlockSpec((nd, tq, tq), lambda bi, hp, i: (0, 0, 0))],
        out_specs=pl.BlockSpec((None, tq, gw), lambda bi, hp, i: (bi, i, hp)),
        out_shape=jax.ShapeDtypeStruct((b, s, w), BF16),
        scratch_shapes=[pltpu.VMEM((n_pairs, 2 * tq, LANE), BF16), state, state, state],
        compiler_params=_cparams(("arbitrary", "arbitrary", "arbitrary")), name="dilated_prompt",
    )(q, k, v, tbl)


def _dil_sample_kernel(qbd_ref, kc_ref, kn_ref, vc_ref, vn_ref, tc_ref, tn_ref, o_ref):
    qbd = qbd_ref[...]
    scale = HEAD_DIM ** -0.5
    sc = _dot(kc_ref[...].astype(BF16), qbd) * scale + tc_ref[...]
    sn = _dot(kn_ref[...].astype(BF16), qbd) * scale + tn_ref[...]
    m = jnp.maximum(jnp.max(sc, axis=0, keepdims=True), jnp.max(sn, axis=0, keepdims=True))
    pc = jnp.exp(sc - m)
    pn = jnp.exp(sn - m)
    l = jnp.sum(pc, axis=0, keepdims=True) + jnp.sum(pn, axis=0, keepdims=True)
    acc = _dot_tn(vc_ref[...].astype(BF16), pc.astype(BF16)) + _dot_tn(vn_ref[...].astype(BF16), pn.astype(BF16))
    o_ref[...] = acc / l


def _dilated_sample(q, k_new, v_new, cache_k, cache_v):
    db, ds, w = q.shape
    wb = cache_k.shape[1]
    cols = N_HEADS_C * ds
    qt = q.reshape(db, ds, N_HEADS_C, HEAD_DIM).transpose(0, 2, 3, 1)
    eye = jnp.eye(N_HEADS_C, dtype=q.dtype)
    qbd = (qt[:, :, :, None, :] * eye[None, :, None, :, None]).reshape(db, w, cols)
    rel = (wb + np.arange(ds)[None, :]) - np.arange(wb + ds)[:, None]
    tbl = np.tile(_mult_bias(rel), (1, N_HEADS_C))
    per_b = lambda *shape: pl.BlockSpec((None,) + shape, lambda b: (b,) + (0,) * len(shape))
    const = lambda a: pl.BlockSpec(a.shape, lambda b: (0,) * a.ndim)
    tc, tn = jnp.asarray(tbl[:wb]), jnp.asarray(tbl[wb:])
    acc_t = pl.pallas_call(
        _dil_sample_kernel, grid=(db,),
        in_specs=[per_b(w, cols), per_b(wb, w), per_b(ds, w), per_b(wb, w), per_b(ds, w), const(tc), const(tn)],
        out_specs=per_b(w, cols),
        out_shape=jax.ShapeDtypeStruct((db, w, cols), F32),
        compiler_params=_cparams(("arbitrary",)), name="dilated_sample",
    )(qbd, cache_k, k_new, cache_v, v_new, tc, tn)
    a5 = acc_t.reshape(db, N_HEADS_C, HEAD_DIM, N_HEADS_C, ds)
    diag = jnp.diagonal(a5, axis1=1, axis2=3)
    return diag.transpose(0, 2, 3, 1).reshape(db, ds, w)


def kernel(x_prompt, x_sample, c_prompt, c_sample, cache_a_k, cache_a_v, cache_a_idx_k, state_conv, cache_c_k, cache_c_v, page_table, norm_mix_g, norm_ffn_g, w_ada, b_ada, w_in_even, w_out_even, a_q_norm_g, a_k_norm_g, b_conv_w, b_conv_b, b_ln_g, b_ln_b, w_in_odd, w_out_odd, c_q_norm_g, c_k_norm_g, w_router, b_router, w_gate_e, w_up_e, w_down_e, w_gate_s, w_up_s, w_down_s):
    bp, sp, d = x_prompt.shape
    bs, ss, _ = x_sample.shape
    depth = w_ada.shape[0]
    past = page_table.shape[1] * cache_a_k.shape[2]
    win_buf = cache_c_k.shape[2]
    rp, rs = bp * sp, bs * ss

    tab_p = _rope_tables(jnp.arange(sp, dtype=I32))
    tab_s = _rope_tables(jnp.tile(past + jnp.arange(ss, dtype=I32), bs))
    row = lambda a: a.reshape(1, -1)
    gain2 = lambda g: jnp.tile(g, 2).reshape(1, LANE)

    xp = x_prompt.reshape(rp, d)
    xs = x_sample.reshape(rs, d)
    c_all = jnp.concatenate([c_prompt, c_sample], axis=0)
    c_all = jnp.pad(c_all, ((0, -c_all.shape[0] % 8), (0, 0)))

    ak_p, av_p, ai_p, ak_s, av_s, ai_s, cv_p, cv_s = [], [], [], [], [], [], [], []
    ck_p, cw_p, ck_s, cw_s = [], [], [], []
    for layer in range(depth):
        mod = _adaln(c_all, w_ada[layer], b_ada[layer])
        mod_p = [m.reshape(bp, 1, d) for m in jnp.split(mod[:bp], 6, axis=-1)]
        mod_s = [jnp.repeat(m, ss, axis=0) for m in jnp.split(mod[bp:bp + bs], 6, axis=-1)]
        gmix, gffn = row(norm_mix_g[layer]), row(norm_ffn_g[layer])
        if layer % 2 == 0:
            e = layer // 2
            w_in = _perm_w_even(w_in_even[e])
            w_out = w_out_even[e].astype(BF16)
            qg, kg = gain2(a_q_norm_g[e]), gain2(a_k_norm_g[e])
            outs = [(A_W, BF16), (A_W, F32), (A_W, BF16), (A_W, F32), (A_W, BF16), (2 * LANE, BF16),
                    (LANE, F32), (LANE, BF16), (LANE, F32), (CONV_CH, F32)]
            q, k, kb, v, vb, qi, ki2, ki2b, wi, u = _project(
                _proj_even_kernel, xp, gmix, mod_p[0], mod_p[1], w_in, tab_p, qg, kg, sp, outs)
            b3 = lambda a: a.reshape(bp, sp, a.shape[-1])
            oa = _dsa_prompt(b3(q), b3(qi), b3(wi), b3(ki2b), b3(kb), b3(vb)).reshape(rp, A_W)
            u3 = b3(u)
            ob = _conv_module(u3, jnp.zeros((bp, CONV_WIDTH - 1, CONV_CH), F32),
                              b_conv_w[e], b_conv_b[e], b_ln_g[e], b_ln_b[e]).reshape(rp, CONV_CH)
            lhs_p = [oa, ob]
            ak_p.append(k.reshape(bp, sp, N_HEADS_A, HEAD_DIM))
            av_p.append(v.reshape(bp, sp, N_HEADS_A, HEAD_DIM))
            ai_p.append(ki2[:, :IDX_DIM].reshape(bp, sp, IDX_DIM))
            xpad = jnp.concatenate([jnp.zeros((bp, CONV_WIDTH - 1, CONV_CH), F32), u3], axis=1)
            cv_p.append(xpad[:, -(CONV_WIDTH - 1):])
            q, k, kb, v, vb, qi, ki2, ki2b, wi, u = _project(
                _proj_even_kernel, xs, gmix, mod_s[0], mod_s[1], w_in, tab_s, qg, kg, rs, outs)
            s3 = lambda a: a.reshape(bs, ss, a.shape[-1])
            oa = _dsa_sample(s3(q), s3(k), s3(v), s3(qi), s3(ki2)[:, :, :IDX_DIM], s3(wi)[:, :, :N_IDX_HEADS],
                             cache_a_k[e], cache_a_v[e], cache_a_idx_k[e], page_table)
            u3 = s3(u)
            ob = _conv_module(u3, state_conv[e], b_conv_w[e], b_conv_b[e], b_ln_g[e], b_ln_b[e])
            lhs_s = [oa.reshape(rs, A_W).astype(BF16), ob.reshape(rs, CONV_CH)]
            ak_s.append(k.reshape(bs, ss, N_HEADS_A, HEAD_DIM))
            av_s.append(v.reshape(bs, ss, N_HEADS_A, HEAD_DIM))
            ai_s.append(ki2[:, :IDX_DIM].reshape(bs, ss, IDX_DIM))
            cv_s.append(jnp.concatenate([state_conv[e], u3], axis=1)[:, -(CONV_WIDTH - 1):])
            ws = [w_out[:A_W], w_out[A_W:]]
        else:
            o = layer // 2
            w_in = w_in_odd[o].astype(BF16)
            qg, kg = gain2(c_q_norm_g[o]), gain2(c_k_norm_g[o])
            outs = [(C_W, BF16), (C_W, F32), (C_W, BF16), (C_W, F32), (C_W, BF16)]
            q, k, kb, v, vb = _project(_proj_odd_kernel, xp, gmix, mod_p[0], mod_p[1], w_in, tab_p, qg, kg, sp, outs)
            b3 = lambda a: a.reshape(bp, sp, a.shape[-1])
            lhs_p = [_dilated_prompt(b3(q), b3(kb), b3(vb)).reshape(rp, C_W)]
            k4, v4 = k.reshape(bp, sp, N_HEADS_C, HEAD_DIM), v.reshape(bp, sp, N_HEADS_C, HEAD_DIM)
            padt = max(win_buf - sp, 0)
            tail = lambda a: jnp.pad(a, ((0, 0), (padt, 0), (0, 0), (0, 0)))[:, -win_buf:]
            ck_p.append(tail(k4))
            cw_p.append(tail(v4))
            q, k, kb, v, vb = _project(_proj_odd_kernel, xs, gmix, mod_s[0], mod_s[1], w_in, tab_s, qg, kg, rs, outs)
            s3 = lambda a: a.reshape(bs, ss, a.shape[-1])
            att = _dilated_sample(s3(q), s3(k), s3(v), cache_c_k[o].reshape(bs, win_buf, C_W),
                                  cache_c_v[o].reshape(bs, win_buf, C_W))
            lhs_s = [att.reshape(rs, C_W).astype(BF16)]
            k4, v4 = k.reshape(bs, ss, N_HEADS_C, HEAD_DIM), v.reshape(bs, ss, N_HEADS_C, HEAD_DIM)
            ck_s.append(jnp.concatenate([cache_c_k[o], k4], axis=1)[:, -win_buf:])
            cw_s.append(jnp.concatenate([cache_c_v[o], v4], axis=1)[:, -win_buf:])
            ws = [w_out_odd[o].astype(BF16)]

        wg, wu, wd = w_gate_e[layer].astype(BF16), w_up_e[layer].astype(BF16), w_down_e[layer].astype(BF16)
        wgs, wus, wds = w_gate_s[layer].astype(BF16), w_up_s[layer].astype(BF16), w_down_s[layer].astype(BF16)
        xn, h, idx, gate = _outproj_router(lhs_p, ws, xp, mod_p[2], gffn, mod_p[3], mod_p[4],
                                           w_router[layer], b_router[layer], sp)
        xp = _moe(h, xn, mod_p[5], idx[:, :TOP_K], gate[:, :TOP_K], wg, wu, wd, wgs, wus, wds, sp)
        xn, h, idx, gate = _outproj_router(lhs_s, ws, xs, mod_s[2], gffn, mod_s[3], mod_s[4],
                                           w_router[layer], b_router[layer], rs)
        xs = _moe(h, xn, mod_s[5], idx[:, :TOP_K], gate[:, :TOP_K], wg, wu, wd, wgs, wus, wds, rs)

    return (xp.reshape(bp, sp, d), xs.reshape(bs, ss, d),
            jnp.stack(ak_p), jnp.stack(av_p), jnp.stack(ai_p),
            jnp.stack(ak_s), jnp.stack(av_s), jnp.stack(ai_s),
            jnp.stack(cv_p), jnp.stack(cv_s),
            jnp.stack(ck_p), jnp.stack(cw_p), jnp.stack(ck_s), jnp.stack(cw_s))
```

```python
import functools

import numpy as np
import jax
import jax.numpy as jnp
from jax import lax
from jax.experimental import pallas as pl
from jax.experimental.pallas import tpu as pltpu

F32, BF16, I32 = jnp.float32, jnp.bfloat16, jnp.int32

HEAD_DIM = 64
N_HEADS_A = 8
CONV_CH = 512
CONV_WIDTH = 31
N_IDX_HEADS = 4
IDX_DIM = 64
TOPK_MAX = 256
N_HEADS_C = 16
DILATED_PATTERNS = ((128, 1), (512, 4), (2048, 16))
N_EXPERTS = 64
TOP_K = 6
ROUTED_SCALE = 2.5
ROPE_THETA = 10000.0
EPS = 1e-6

LANE = 128
ROW_GROUP = 8
NEG = -1e30
KEY_NEG_INF = -2139095041
VMEM_LIMIT = 60 * 1024 * 1024

A_W = N_HEADS_A * HEAD_DIM
C_W = N_HEADS_C * HEAD_DIM


def _cparams(sem):
    return pltpu.CompilerParams(dimension_semantics=sem, vmem_limit_bytes=VMEM_LIMIT)


def _dot(a, b):
    return jnp.dot(a, b, preferred_element_type=F32)


def _dot_nt(a, b):
    return lax.dot_general(a, b, (((1,), (1,)), ((), ())), preferred_element_type=F32)


def _dot_tn(a, b):
    return lax.dot_general(a, b, (((0,), (0,)), ((), ())), preferred_element_type=F32)


def _silu(x):
    return x * jax.nn.sigmoid(x)


def _rms_mod(x, g, shift, scale):
    ms = jnp.mean(x * x, axis=-1, keepdims=True)
    y = x * lax.rsqrt(ms + EPS) * g
    return y * (1.0 + scale) + shift


def _to_key(x):
    b = lax.bitcast_convert_type(x, I32)
    k = jnp.where(b < 0, b ^ 0x7FFFFFFF, b)
    return jnp.where(k == -1, 0, k)


def _rope_tables(pos):
    half = HEAD_DIM // 2
    inv = ROPE_THETA ** (-jnp.arange(half, dtype=F32) / half)
    ang = pos.astype(F32)[:, None] * inv[None, :]
    c, s = jnp.cos(ang), jnp.sin(ang)
    z = jnp.zeros_like(s)
    cos = jnp.tile(c, (1, 4))
    sin_a = jnp.tile(jnp.concatenate([z, s], axis=1), (1, 2))
    sin_b = jnp.tile(jnp.concatenate([-s, z], axis=1), (1, 2))
    return cos, sin_a, sin_b


def _head_sum_matrix():
    r = np.arange(LANE)
    return jnp.asarray((r[:, None] // HEAD_DIM == r[None, :] // HEAD_DIM).astype(np.float32), BF16)


def _prefix_matrix(n):
    r = np.arange(n)
    return jnp.asarray((r[:, None] <= r[None, :]).astype(np.float32), BF16)


def _mult_bias(rel):
    mult = np.zeros(rel.shape, np.int64)
    for w, d in DILATED_PATTERNS:
        mult += ((rel >= 0) & (rel <= w) & (rel % d == 0))
    return np.where(mult > 0, np.log(np.maximum(mult, 1)), NEG).astype(np.float32)


def _ada_kernel(c_ref, w_ref, b_ref, o_ref):
    a = _silu(c_ref[...]).astype(BF16)
    o_ref[...] = _dot(a, w_ref[...].astype(BF16)) + b_ref[...]


def _adaln(c, w, b):
    m, d = c.shape
    n = w.shape[1]
    tn = 1536
    return pl.pallas_call(
        _ada_kernel, grid=(n // tn,),
        in_specs=[pl.BlockSpec((m, d), lambda j: (0, 0)),
                  pl.BlockSpec((d, tn), lambda j: (0, j)),
                  pl.BlockSpec((1, tn), lambda j: (0, j))],
        out_specs=pl.BlockSpec((m, tn), lambda j: (0, j)),
        out_shape=jax.ShapeDtypeStruct((m, n), F32),
        compiler_params=_cparams(("arbitrary",)), name="adaln")(c, w, b.reshape(1, n))


def _norm_rope(z, bsum, g, cos, sin_a, sin_b, norm):
    if norm:
        sq = z * z
        hi = sq.astype(BF16)
        lo = (sq - hi.astype(F32)).astype(BF16)
        ms = (_dot(hi, bsum) + _dot(lo, bsum)) * (1.0 / HEAD_DIM)
        z = z * lax.rsqrt(ms + EPS) * g
    return z * cos + pltpu.roll(z, 32, 1) * sin_a + pltpu.roll(z, 96, 1) * sin_b


def _proj_even_kernel(x_ref, gn_ref, sh_ref, sc_ref, w_ref, cos_ref, sa_ref, sb_ref, qg_ref, kg_ref, bsum_ref,
                      q_ref, k_ref, kb_ref, v_ref, vb_ref, qi_ref, ki_ref, kib_ref, wi_ref, u_ref, h_scr):
    h_scr[...] = _rms_mod(x_ref[...], gn_ref[...], sh_ref[...], sc_ref[...]).astype(BF16)
    hb = h_scr[...]
    cos, sa, sb, bsum = cos_ref[...], sa_ref[...], sb_ref[...], bsum_ref[...]

    def seg(lo, n):
        return _dot(hb, w_ref[:, lo:lo + n])

    z = seg(0, A_W)
    for c in range(A_W // LANE):
        sl = slice(c * LANE, (c + 1) * LANE)
        q_ref[:, sl] = _norm_rope(z[:, sl], bsum, qg_ref[...], cos, sa, sb, True).astype(BF16)
    z = seg(A_W, A_W)
    for c in range(A_W // LANE):
        sl = slice(c * LANE, (c + 1) * LANE)
        kk = _norm_rope(z[:, sl], bsum, kg_ref[...], cos, sa, sb, True)
        k_ref[:, sl] = kk
        kb_ref[:, sl] = kk.astype(BF16)
    z = seg(2 * A_W, A_W)
    v_ref[...] = z
    vb_ref[...] = z.astype(BF16)
    z = seg(3 * A_W, 2 * LANE)
    for c in range(2):
        sl = slice(c * LANE, (c + 1) * LANE)
        qi_ref[:, sl] = _norm_rope(z[:, sl], bsum, None, cos, sa, sb, False).astype(BF16)
    z = _norm_rope(seg(3 * A_W + 2 * LANE, LANE), bsum, None, cos, sa, sb, False)
    ki_ref[...] = z
    kib_ref[...] = z.astype(BF16)
    off = 3 * A_W + 3 * LANE
    ua = seg(off, CONV_CH)
    ub = seg(off + CONV_CH, CONV_CH)
    u_ref[...] = ua * jax.nn.sigmoid(ub)
    wi_ref[...] = seg(off + 2 * CONV_CH, LANE) * (N_IDX_HEADS ** -0.5 * IDX_DIM ** -0.5)


def _proj_odd_kernel(x_ref, gn_ref, sh_ref, sc_ref, w_ref, cos_ref, sa_ref, sb_ref, qg_ref, kg_ref, bsum_ref,
                     q_ref, k_ref, kb_ref, v_ref, vb_ref, h_scr):
    h_scr[...] = _rms_mod(x_ref[...], gn_ref[...], sh_ref[...], sc_ref[...]).astype(BF16)
    hb = h_scr[...]
    cos, sa, sb, bsum = cos_ref[...], sa_ref[...], sb_ref[...], bsum_ref[...]
    for half in range(2):
        z = _dot(hb, w_ref[:, half * A_W:(half + 1) * A_W])
        for c in range(A_W // LANE):
            sl = slice(c * LANE, (c + 1) * LANE)
            dl = slice(half * A_W + c * LANE, half * A_W + (c + 1) * LANE)
            q_ref[:, dl] = _norm_rope(z[:, sl], bsum, qg_ref[...], cos, sa, sb, True).astype(BF16)
    for half in range(2):
        z = _dot(hb, w_ref[:, C_W + half * A_W:C_W + (half + 1) * A_W])
        for c in range(A_W // LANE):
            sl = slice(c * LANE, (c + 1) * LANE)
            dl = slice(half * A_W + c * LANE, half * A_W + (c + 1) * LANE)
            kk = _norm_rope(z[:, sl], bsum, kg_ref[...], cos, sa, sb, True)
            k_ref[:, dl] = kk
            kb_ref[:, dl] = kk.astype(BF16)
    for half in range(2):
        z = _dot(hb, w_ref[:, 2 * C_W + half * A_W:2 * C_W + (half + 1) * A_W])
        v_ref[:, half * A_W:(half + 1) * A_W] = z
        vb_ref[:, half * A_W:(half + 1) * A_W] = z.astype(BF16)


def _mod_spec(mod, tr, tiles_per_group):
    d = mod.shape[-1]
    if mod.ndim == 3:
        return pl.BlockSpec((None, 1, d), lambda i: (i // tiles_per_group, 0, 0))
    return pl.BlockSpec((tr, d), lambda i: (i, 0))


def _row_tile(rows, want):
    return want if rows % want == 0 else rows


def _project(kern, x, gn, shift, scale, w, tables, qg, kg, group_rows, outs):
    r, d = x.shape
    tr = _row_tile(min(r, group_rows), 512)
    tpg = max(group_rows // tr, 1)
    tpt = tables[0].shape[0] // tr
    const = lambda a: pl.BlockSpec(a.shape, lambda i: (0,) * a.ndim)
    tab = pl.BlockSpec((tr, LANE), lambda i: (i % tpt, 0))
    bsum = _head_sum_matrix()
    return pl.pallas_call(
        kern, grid=(r // tr,),
        in_specs=[pl.BlockSpec((tr, d), lambda i: (i, 0)), const(gn), _mod_spec(shift, tr, tpg),
                  _mod_spec(scale, tr, tpg), const(w), tab, tab, tab, const(qg), const(kg), const(bsum)],
        out_specs=[pl.BlockSpec((tr, wd), lambda i: (i, 0)) for wd, _ in outs],
        out_shape=[jax.ShapeDtypeStruct((r, wd), dt) for wd, dt in outs],
        scratch_shapes=[pltpu.VMEM((tr, d), BF16)],
        compiler_params=_cparams(("arbitrary",)), name=kern.__name__.strip("_"),
    )(x, gn, shift, scale, w, *tables, qg, kg, bsum)


def _perm_w_even(w):
    d = w.shape[0]
    cuts = np.cumsum([A_W, A_W, A_W, N_IDX_HEADS * IDX_DIM, IDX_DIM, N_IDX_HEADS, CONV_CH])
    q, k, v, qi, ki, wi, ua, ub = jnp.split(w, [int(c) for c in cuts], axis=1)
    pad = jnp.zeros((d, LANE - N_IDX_HEADS), w.dtype)
    return jnp.concatenate([q, k, v, qi, ki, ki, ua, ub, wi, pad], axis=1).astype(BF16)


def _dsa_prompt_kernel(q_ref, qi_ref, wi_ref, ki_ref, k_ref, v_ref, u_ref, o_ref, key_ref, bias_ref,
                       qs_ref, m_ref, l_ref, acc_ref, *, tq, kc, topk):
    i = pl.program_id(1)
    q0 = i * tq
    cd = q0 // kc
    lane = lax.broadcasted_iota(I32, (tq, LANE), 1)
    lo_half = lane < HEAD_DIM
    zero = jnp.zeros((tq, LANE), BF16)

    qi = qi_ref[...]
    qm = [jnp.where(lo_half if h % 2 == 0 else ~lo_half, qi[:, (h // 2) * LANE:(h // 2 + 1) * LANE], zero)
          for h in range(N_IDX_HEADS)]
    wi = wi_ref[...]
    wcol = [wi[:, h:h + 1] for h in range(N_IDX_HEADS)]

    def score_chunk(c):
        kk = ki_ref[pl.ds(pl.multiple_of(c * kc, kc), kc), :]
        acc = None
        for h in range(N_IDX_HEADS):
            t = wcol[h] * jnp.maximum(_dot_nt(qm[h], kk), 0.0)
            acc = t if acc is None else acc + t
        return acc

    def fill(c, _):
        key_ref[c] = _to_key(score_chunk(c))
        return 0

    lax.fori_loop(0, cd, fill, 0)
    kpos = cd * kc + lax.broadcasted_iota(I32, (tq, kc), 1)
    qpos = q0 + lax.broadcasted_iota(I32, (tq, kc), 0)
    key_ref[cd] = _to_key(jnp.where(kpos <= qpos, score_chunk(cd), -jnp.inf))

    def count(cand, strict):
        def body(c, cnt):
            kk = key_ref[c]
            one = jnp.where((kk > cand) if strict else (kk >= cand), 1.0, 0.0)
            part = one[:, 0:LANE]
            for j in range(1, kc // LANE):
                part = part + one[:, j * LANE:(j + 1) * LANE]
            return cnt + part
        cnt = lax.fori_loop(0, cd + 1, body, jnp.zeros((tq, LANE), F32))
        return jnp.sum(cnt, axis=1, keepdims=True)

    kf = float(topk)
    thr = jnp.where(count(jnp.zeros((tq, 1), I32), False) >= kf, 0, -2 ** 31).astype(I32)

    def bit(b, thr):
        cand = thr + jnp.left_shift(jnp.int32(1), 30 - b)
        return jnp.where(count(cand, False) >= kf, cand, thr)

    thr = lax.fori_loop(0, 31, bit, thr)
    need = kf - count(thr, True)
    upper = u_ref[...]

    def select(c, run):
        kk = key_ref[c]
        tie = kk == thr
        pre = _dot(jnp.where(tie, 1.0, 0.0).astype(BF16), upper) + run
        sel = ((kk > thr) | (tie & (pre <= need))) & (kk > KEY_NEG_INF)
        bias_ref[c] = jnp.where(sel, 0.0, NEG)
        return pre[:, kc - 1:kc]

    lax.fori_loop(0, cd + 1, select, jnp.zeros((tq, 1), F32))

    n_pairs = A_W // LANE
    for hp in range(n_pairs):
        qs_ref[hp] = _stack_pair(q_ref[:, hp * LANE:(hp + 1) * LANE], lo_half)
    _flash_init(m_ref, l_ref, acc_ref)

    def attend(c, _):
        rows = pl.ds(pl.multiple_of(c * kc, kc), kc)
        b = bias_ref[c]
        bias2 = jnp.concatenate([b, b], axis=0)
        for hp in range(n_pairs):
            cols = slice(hp * LANE, (hp + 1) * LANE)
            _flash_step(qs_ref[hp], k_ref[rows, cols], v_ref[rows, cols], bias2, m_ref, l_ref, acc_ref, hp)
        return 0

    lax.fori_loop(0, cd + 1, attend, 0)
    for hp in range(n_pairs):
        o_ref[:, hp * LANE:(hp + 1) * LANE] = _flash_finish(l_ref, acc_ref, hp, lo_half, tq).astype(o_ref.dtype)


def _stack_pair(qp, lo_half):
    qsc = (qp.astype(F32) * (HEAD_DIM ** -0.5)).astype(BF16)
    zero = jnp.zeros_like(qsc)
    return jnp.concatenate([jnp.where(lo_half, qsc, zero), jnp.where(lo_half, zero, qsc)], axis=0)


def _flash_init(m_ref, l_ref, acc_ref):
    m_ref[...] = jnp.full(m_ref.shape, NEG, F32)
    l_ref[...] = jnp.zeros(l_ref.shape, F32)
    acc_ref[...] = jnp.zeros(acc_ref.shape, F32)


def _flash_step(qs, kblk, vblk, bias2, m_ref, l_ref, acc_ref, idx):
    s = _dot_nt(qs, kblk) + bias2
    m_old = m_ref[idx]
    mn = jnp.maximum(m_old, jnp.max(s, axis=1, keepdims=True))
    p = jnp.exp(s - jnp.tile(mn, (1, s.shape[1] // LANE)))
    a = jnp.exp(m_old - mn)
    l_ref[idx] = a * l_ref[idx] + jnp.sum(p, axis=1, keepdims=True)
    acc_ref[idx] = a * acc_ref[idx] + _dot(p.astype(BF16), vblk)
    m_ref[idx] = mn


def _flash_finish(l_ref, acc_ref, idx, lo_half, t):
    o = acc_ref[idx] / l_ref[idx]
    return jnp.where(lo_half, o[:t], o[t:])


def _dsa_prompt(q, qi, wi, ki2, k, v):
    b, s, _ = q.shape
    tq = 128
    kc = min(512, s)
    topk = min(TOPK_MAX, s // 4)
    nc = s // kc
    upper = _prefix_matrix(kc)
    qspec = lambda w: pl.BlockSpec((None, tq, w), lambda bi, i: (bi, i, 0))
    kspec = lambda w: pl.BlockSpec((None, s, w), lambda bi, i: (bi, 0, 0))
    return pl.pallas_call(
        functools.partial(_dsa_prompt_kernel, tq=tq, kc=kc, topk=topk),
        grid=(b, s // tq),
        in_specs=[qspec(A_W), qspec(2 * LANE), qspec(LANE), kspec(LANE), kspec(A_W), kspec(A_W),
                  pl.BlockSpec((kc, kc), lambda bi, i: (0, 0))],
        out_specs=qspec(A_W),
        out_shape=jax.ShapeDtypeStruct((b, s, A_W), BF16),
        scratch_shapes=[pltpu.VMEM((nc, tq, kc), I32), pltpu.VMEM((nc, tq, kc), F32),
                        pltpu.VMEM((A_W // LANE, 2 * tq, LANE), BF16)]
        + [pltpu.VMEM((A_W // LANE, 2 * tq, LANE), F32)] * 3,
        compiler_params=_cparams(("arbitrary", "arbitrary")), name="dsa_prompt",
    )(q, qi, wi, ki2, k, v, upper)


def _dsa_sample_select_kernel(pt_ref, qi_ref, wi_ref, *rest, n_new, npg, pg, topk):
    del pt_ref
    kip_refs = rest[:pg]
    kin_ref, u_ref, o_ref, key_ref = rest[pg:]
    p = pl.program_id(1)
    qi = qi_ref[...]
    wi = wi_ref[...]
    n_chunks = key_ref.shape[0]

    def score(dots):
        r = jnp.maximum(dots, 0.0) * wi
        acc = r[0:n_new]
        for h in range(1, N_IDX_HEADS):
            acc = acc + r[h * n_new:(h + 1) * n_new]
        return acc

    for g in range(pg):
        key_ref[p * pg + g] = _to_key(score(_dot(qi, kip_refs[g][...].astype(BF16))))

    @pl.when(p == 0)
    def _():
        for c in range(npg + 1, n_chunks):
            key_ref[c] = jnp.full((n_new, LANE), -2 ** 31, I32)

    @pl.when(p == npg // pg - 1)
    def _():
        col = lax.broadcasted_iota(I32, (n_new, LANE), 1)
        row = lax.broadcasted_iota(I32, (n_new, LANE), 0)
        new = score(_dot_nt(qi, kin_ref[...].astype(BF16)))
        key_ref[npg] = _to_key(jnp.where(col <= row, new, -jnp.inf))

        def count(cand, strict):
            def body(c, cnt):
                kk = key_ref[pl.ds(pl.multiple_of(c * ROW_GROUP, ROW_GROUP), ROW_GROUP)]
                hit = jnp.where((kk > cand) if strict else (kk >= cand), 1.0, 0.0)
                return cnt + jnp.sum(hit, axis=0)
            cnt = lax.fori_loop(0, n_chunks // ROW_GROUP, body, jnp.zeros((n_new, LANE), F32))
            return jnp.sum(cnt, axis=1, keepdims=True)

        kf = float(topk)
        thr = jnp.where(count(jnp.zeros((n_new, 1), I32), False) >= kf, 0, -2 ** 31).astype(I32)

        def bit(b, thr):
            cand = thr + jnp.left_shift(jnp.int32(1), 30 - b)
            return jnp.where(count(cand, False) >= kf, cand, thr)

        thr = lax.fori_loop(0, 31, bit, thr)
        need = kf - count(thr, True)
        upper = u_ref[...]

        def select(g, run):
            c0 = pl.multiple_of(g * ROW_GROUP, ROW_GROUP)
            kk = key_ref[pl.ds(c0, ROW_GROUP)]
            tie = kk == thr
            ties = jnp.where(tie, 1.0, 0.0).reshape(ROW_GROUP * n_new, LANE).astype(BF16)
            pre = _dot(ties, upper).reshape(ROW_GROUP, n_new, LANE)
            for c in range(ROW_GROUP):
                pc = pre[c] + run
                sel = ((kk[c] > thr) | (tie[c] & (pc <= need))) & (kk[c] > KEY_NEG_INF)
                o_ref[c0 + c] = jnp.where(sel, 1.0, 0.0)
                run = pc[:, LANE - 1:LANE]
            return run

        lax.fori_loop(0, n_chunks // ROW_GROUP, select, jnp.zeros((n_new, 1), F32))


def _block_diag_rows(o_full, n_heads, n_new):
    head_of_lane = lax.broadcasted_iota(I32, (n_new, o_full.shape[1]), 1) // HEAD_DIM
    out = jnp.zeros((n_new, o_full.shape[1]), F32)
    for h in range(n_heads):
        out = jnp.where(head_of_lane == h, o_full[h * n_new:(h + 1) * n_new], out)
    return out


def _dsa_sample_attend_kernel(pt_ref, qbd_ref, *rest, npg, pg, n_new):
    del pt_ref
    kp_refs, vp_refs = rest[:pg], rest[pg:2 * pg]
    kn_ref, vn_ref, sel_ref, o_ref, m_ref, l_ref, acc_ref = rest[2 * pg:]
    p = pl.program_id(1)
    qbd = qbd_ref[...]

    @pl.when(p == 0)
    def _():
        _flash_init(m_ref, l_ref, acc_ref)

    def bias(sel):
        return (jnp.concatenate([sel] * N_HEADS_A, axis=0) - 1.0) * (-NEG)

    def update(scores, pv):
        m = m_ref[...]
        mn = m
        for s in scores:
            mn = jnp.maximum(mn, jnp.max(s, axis=1, keepdims=True))
        a = jnp.exp(m - mn)
        l = a * l_ref[...]
        acc = jnp.tile(a, (1, A_W // LANE)) * acc_ref[...]
        for s, f in zip(scores, pv):
            pr = jnp.exp(s - mn)
            l = l + jnp.sum(pr, axis=1, keepdims=True)
            acc = acc + f(pr.astype(BF16))
        m_ref[...] = mn
        l_ref[...] = l
        acc_ref[...] = acc

    scale = HEAD_DIM ** -0.5
    update([_dot(qbd, kp_refs[g][...].astype(BF16)) * scale + bias(sel_ref[p * pg + g]) for g in range(pg)],
           [lambda pr, g=g: _dot_nt(pr, vp_refs[g][...].astype(BF16)) for g in range(pg)])

    @pl.when(p == npg // pg - 1)
    def _():
        update([_dot_nt(qbd, kn_ref[...].astype(BF16)) * scale + bias(sel_ref[npg])],
               [lambda pr: _dot(pr, vn_ref[...].astype(BF16))])
        o_full = acc_ref[...] / jnp.tile(l_ref[...], (1, A_W // LANE))
        o_ref[...] = _block_diag_rows(o_full, N_HEADS_A, n_new)


def _dsa_sample(q, k_new, v_new, qi, ki_new, wi, pool_k, pool_v, pool_ki, page_table):
    db, ds, _ = q.shape
    n_phys, page = pool_k.shape[0], pool_k.shape[1]
    npg = page_table.shape[1]
    topk = min(TOPK_MAX, (npg * page + ds) // 4)
    pt = page_table.reshape(-1).astype(I32)
    hq = N_IDX_HEADS * ds
    qi_r = qi.reshape(db, ds, N_IDX_HEADS, IDX_DIM).transpose(0, 2, 1, 3).reshape(db, hq, IDX_DIM)
    wi_r = jnp.broadcast_to(wi.transpose(0, 2, 1).reshape(db, hq, 1), (db, hq, LANE))
    pad = lambda a: jnp.pad(a, ((0, 0), (0, page - ds), (0, 0)))
    per_b = lambda *shape: pl.BlockSpec((None,) + shape, lambda b, p, pt: (b,) + (0,) * len(shape))
    pg = next(g for g in (8, 4, 2, 1) if npg % g == 0)
    paged = lambda w: [pl.BlockSpec((None, w, page), lambda b, p, pt, g=g: (pt[b * npg + p * pg + g], 0, 0))
                       for g in range(pg)]
    pool_kt = jnp.transpose(pool_k, (0, 2, 3, 1)).reshape(n_phys, A_W, page)
    pool_vt = jnp.transpose(pool_v, (0, 2, 3, 1)).reshape(n_phys, A_W, page)
    pool_kit = jnp.transpose(pool_ki, (0, 2, 1))
    n_chunks = -(-(npg + 1) // ROW_GROUP) * ROW_GROUP
    sel = pl.pallas_call(
        functools.partial(_dsa_sample_select_kernel, n_new=ds, npg=npg, pg=pg, topk=topk),
        grid_spec=pltpu.PrefetchScalarGridSpec(
            num_scalar_prefetch=1, grid=(db, npg // pg),
            in_specs=[per_b(hq, IDX_DIM), per_b(hq, LANE)] + paged(IDX_DIM)
            + [per_b(page, IDX_DIM), pl.BlockSpec((page, page), lambda b, p, pt: (0, 0))],
            out_specs=per_b(n_chunks, ds, page),
            scratch_shapes=[pltpu.VMEM((n_chunks, ds, page), I32)]),
        out_shape=jax.ShapeDtypeStruct((db, n_chunks, ds, page), F32),
        compiler_params=_cparams(("arbitrary", "arbitrary")), name="dsa_sample_select",
    )(pt, qi_r, wi_r, *([pool_kit] * pg), pad(ki_new), _prefix_matrix(page))

    rows = N_HEADS_A * ds
    qh = q.reshape(db, ds, N_HEADS_A, HEAD_DIM).transpose(0, 2, 1, 3)
    eye = jnp.eye(N_HEADS_A, dtype=q.dtype)
    qbd = (qh[:, :, :, None, :] * eye[None, :, None, :, None]).reshape(db, rows, A_W)
    state = pltpu.VMEM((rows, LANE), F32)
    return pl.pallas_call(
        functools.partial(_dsa_sample_attend_kernel, npg=npg, pg=pg, n_new=ds),
        grid_spec=pltpu.PrefetchScalarGridSpec(
            num_scalar_prefetch=1, grid=(db, npg // pg),
            in_specs=[per_b(rows, A_W)] + paged(A_W) + paged(A_W)
            + [per_b(page, A_W), per_b(page, A_W), per_b(n_chunks, ds, page)],
            out_specs=per_b(ds, A_W),
            scratch_shapes=[state, state, pltpu.VMEM((rows, A_W), F32)]),
        out_shape=jax.ShapeDtypeStruct((db, ds, A_W), F32),
        compiler_params=_cparams(("arbitrary", "arbitrary")), name="dsa_sample_attend",
    )(pt, qbd, *([pool_kt] * pg), *([pool_vt] * pg), pad(k_new), pad(v_new), sel)


def _conv_kernel(u_ref, halo_ref, buf_ref, w_ref, b_ref, g_ref, bb_ref, o_ref, xp_ref, *, tt):
    i = pl.program_id(1)

    @pl.when(i == 0)
    def _():
        xp_ref[0:32, :] = buf_ref[...]

    @pl.when(i > 0)
    def _():
        xp_ref[0:32, :] = halo_ref[...]

    xp_ref[32:32 + tt, :] = u_ref[...]
    w = w_ref[...]
    acc = jnp.zeros((tt, CONV_CH), F32) + b_ref[...]
    for j in range(CONV_WIDTH):
        acc = acc + xp_ref[2 + j:2 + j + tt, :] * w[j:j + 1, :]
    mu = jnp.mean(acc, axis=-1, keepdims=True)
    var = jnp.mean(jnp.square(acc - mu), axis=-1, keepdims=True)
    yn = (acc - mu) * lax.rsqrt(var + EPS) * g_ref[...] + bb_ref[...]
    o_ref[...] = _silu(yn).astype(o_ref.dtype)


def _conv_module(u, buf30, conv_w, conv_b, ln_g, ln_b):
    b, t, c = u.shape
    tt = _row_tile(t, 512)
    buf = jnp.pad(buf30, ((0, 0), (2, 0), (0, 0)))
    halo_src = u if t >= 32 else buf
    hb = tt // 32
    w = jnp.pad(conv_w, ((0, 32 - CONV_WIDTH), (0, 0)))
    const = lambda a: pl.BlockSpec(a.shape, lambda bi, i: (0,) * a.ndim)
    row = lambda a: a.reshape(1, c)
    return pl.pallas_call(
        functools.partial(_conv_kernel, tt=tt), grid=(b, t // tt),
        in_specs=[pl.BlockSpec((None, tt, c), lambda bi, i: (bi, i, 0)),
                  pl.BlockSpec((None, 32, c), lambda bi, i: (bi, jnp.maximum(i * hb - 1, 0), 0)),
                  pl.BlockSpec((None, 32, c), lambda bi, i: (bi, 0, 0)),
                  const(w), const(row(conv_b)), const(row(ln_g)), const(row(ln_b))],
        out_specs=pl.BlockSpec((None, tt, c), lambda bi, i: (bi, i, 0)),
        out_shape=jax.ShapeDtypeStruct((b, t, c), BF16),
        scratch_shapes=[pltpu.VMEM((32 + tt, c), F32)],
        compiler_params=_cparams(("arbitrary", "arbitrary")), name="conv_module",
    )(u, halo_src, buf, w, row(conv_b), row(ln_g), row(ln_b))


def _outproj_kernel(*refs, n_lhs):
    lhs, ws = refs[:n_lhs], refs[n_lhs:2 * n_lhs]
    x_ref, g2_ref, gn_ref, sh_ref, sc_ref, wr_ref, br_ref, xn_ref, h_ref, idx_ref, gate_ref = refs[2 * n_lhs:]
    mix = _dot(lhs[0][...], ws[0][...])
    for a, w in zip(lhs[1:], ws[1:]):
        mix = mix + _dot(a[...], w[...])
    xn = x_ref[...] + g2_ref[...] * mix
    xn_ref[...] = xn
    h = _rms_mod(xn, gn_ref[...], sh_ref[...], sc_ref[...])
    h_ref[...] = h
    s = jax.nn.sigmoid(_dot(h.astype(BF16), wr_ref[...]))
    tr = s.shape[0]
    lane = lax.broadcasted_iota(I32, (tr, LANE), 1)
    lanef = lane.astype(F32)
    vals = jnp.where(lane < N_EXPERTS, s + br_ref[...], -jnp.inf)
    idxm = jnp.zeros((tr, LANE), F32)
    gm = jnp.zeros((tr, LANE), F32)
    for j in range(TOP_K):
        m = jnp.max(vals, axis=1, keepdims=True)
        ix = jnp.min(jnp.where(vals == m, lanef, float(LANE)), axis=1, keepdims=True)
        hit = lanef == ix
        sj = jnp.sum(jnp.where(hit, s, 0.0), axis=1, keepdims=True)
        idxm = jnp.where(lane == j, ix, idxm)
        gm = jnp.where(lane == j, sj, gm)
        vals = jnp.where(hit, -jnp.inf, vals)
    idx_ref[...] = idxm.astype(I32)
    gate_ref[...] = gm / jnp.sum(gm, axis=1, keepdims=True) * ROUTED_SCALE


def _outproj_router(lhs, ws, x, gate2, gn, shift, scale, w_router, b_router, group_rows):
    r, d = x.shape
    tr = _row_tile(min(r, group_rows), 512)
    tpg = max(group_rows // tr, 1)
    const = lambda a: pl.BlockSpec(a.shape, lambda i: (0,) * a.ndim)
    rowblk = lambda w: pl.BlockSpec((tr, w), lambda i: (i, 0))
    wr = jnp.pad(w_router, ((0, 0), (0, LANE - N_EXPERTS))).astype(BF16)
    br = jnp.pad(b_router, (0, LANE - N_EXPERTS)).reshape(1, LANE)
    return pl.pallas_call(
        functools.partial(_outproj_kernel, n_lhs=len(lhs)), grid=(r // tr,),
        in_specs=[rowblk(a.shape[1]) for a in lhs] + [const(w) for w in ws]
        + [rowblk(d), _mod_spec(gate2, tr, tpg), const(gn), _mod_spec(shift, tr, tpg), _mod_spec(scale, tr, tpg),
           const(wr), const(br)],
        out_specs=[rowblk(d), rowblk(d), rowblk(LANE), rowblk(LANE)],
        out_shape=[jax.ShapeDtypeStruct((r, d), F32), jax.ShapeDtypeStruct((r, d), F32),
                   jax.ShapeDtypeStruct((r, LANE), I32), jax.ShapeDtypeStruct((r, LANE), F32)],
        compiler_params=_cparams(("arbitrary",)), name="outproj_router",
    )(*lhs, *ws, x, gate2, gn, shift, scale, wr, br)


def _moe_kernel(cnt_ref, off_ref, tok_ref, gate_ref, h_ref, x_ref, g5_ref, wg_ref, wu_ref, wd_ref,
                wgs_ref, wus_ref, wds_ref, o_ref, acc_ref, xg_ref, y_ref, *, t_rows, ch):
    t = pl.program_id(0)
    e = pl.program_id(1)
    ne = pl.num_programs(1)

    def ffn(xb, wg, wu, wd):
        a = _silu(_dot(xb, wg)) * _dot(xb, wu)
        return _dot(a.astype(BF16), wd)

    @pl.when(e == 0)
    def _():
        xg_ref[...] = jnp.zeros(xg_ref.shape, F32)
        acc_ref[t_rows:t_rows + ROW_GROUP, :] = jnp.zeros((ROW_GROUP, acc_ref.shape[1]), F32)

        def shared(c, _):
            rows = pl.ds(pl.multiple_of(c * ch, ch), ch)
            acc_ref[rows, :] = ffn(h_ref[rows, :].astype(BF16), wgs_ref[...], wus_ref[...], wds_ref[...])
            return 0

        lax.fori_loop(0, t_rows // ch, shared, 0)

    n = cnt_ref[t * ne + e]
    base = off_ref[t * ne + e]

    def chunk(c, _):
        r0 = base + c * ch
        rows = jnp.minimum(ch, n - c * ch)
        groups = lax.shift_right_logical(rows + (ROW_GROUP - 1), ROW_GROUP.bit_length() - 1)

        def gather(g, _):
            r = r0 + g * ROW_GROUP
            picked = [h_ref[pl.ds(tok_ref[r + j], 1), :] for j in range(ROW_GROUP)]
            xg_ref[pl.ds(pl.multiple_of(g * ROW_GROUP, ROW_GROUP), ROW_GROUP), :] = jnp.concatenate(picked, axis=0)
            return 0

        lax.fori_loop(0, groups, gather, 0)
        y_ref[...] = ffn(xg_ref[...].astype(BF16), wg_ref[...], wu_ref[...], wd_ref[...])

        def scatter(g, _):
            y8 = y_ref[pl.ds(pl.multiple_of(g * ROW_GROUP, ROW_GROUP), ROW_GROUP), :]
            dst, gate = [], []
            for j in range(ROW_GROUP):
                r = g * ROW_GROUP + j
                ok = r < rows
                dst.append(jnp.where(ok, tok_ref[r0 + r], t_rows + j))
                gate.append(jnp.where(ok, gate_ref[r0 + r], 0.0))
            new = [acc_ref[pl.ds(dst[j], 1), :] + gate[j] * y8[j:j + 1, :] for j in range(ROW_GROUP)]
            for j in range(ROW_GROUP):
                acc_ref[pl.ds(dst[j], 1), :] = new[j]
            return 0

        lax.fori_loop(0, groups, scatter, 0)
        return 0

    lax.fori_loop(0, (n + ch - 1) // ch, chunk, 0)

    @pl.when(e == ne - 1)
    def _():
        o_ref[...] = x_ref[...] + g5_ref[...] * acc_ref[0:t_rows, :]


def _moe(h, x, gate5, idx6, g6, wg, wu, wd, wgs, wus, wds, group_rows):
    r, d = h.shape
    t_rows = _row_tile(min(r, group_rows), 1024)
    ch = min(128, t_rows)
    nt = r // t_rows
    per = t_rows * TOP_K
    tpg = max(group_rows // t_rows, 1)
    e = idx6.reshape(nt, per)
    order = jnp.argsort(e, axis=-1, stable=True)
    per_pad = -(-(per + ROW_GROUP) // 1024) * 1024
    padl = lambda a: jnp.pad(a, ((0, 0), (0, per_pad - per))).reshape(-1)
    tok = padl((order // TOP_K).astype(I32))
    gs = padl(jnp.take_along_axis(g6.reshape(nt, per), order, axis=-1))
    sorted_e = jnp.take_along_axis(e, order, axis=-1)
    bounds = jax.vmap(lambda se: jnp.searchsorted(se, jnp.arange(N_EXPERTS + 1, dtype=I32)))(sorted_e).astype(I32)
    off = bounds[:, :-1].reshape(-1)
    cnt = (bounds[:, 1:] - bounds[:, :-1]).reshape(-1)
    if gate5.ndim == 3:
        g5spec = pl.BlockSpec((None, 1, d), lambda ti, ei, *_: (ti // tpg, 0, 0))
    else:
        g5spec = pl.BlockSpec((t_rows, d), lambda ti, ei, *_: (ti, 0))
    tile = pl.BlockSpec((t_rows, d), lambda ti, ei, *_: (ti, 0))
    smem = pl.BlockSpec((per_pad,), lambda ti, ei, *_: (ti,), memory_space=pltpu.SMEM)
    expert = lambda a: pl.BlockSpec((None,) + a.shape[1:], lambda ti, ei, *_: (ei, 0, 0))
    const = lambda a: pl.BlockSpec(a.shape, lambda ti, ei, *_: (0,) * a.ndim)
    return pl.pallas_call(
        functools.partial(_moe_kernel, t_rows=t_rows, ch=ch),
        grid_spec=pltpu.PrefetchScalarGridSpec(
            num_scalar_prefetch=2, grid=(nt, N_EXPERTS),
            in_specs=[smem, smem, tile, tile, g5spec, expert(wg), expert(wu), expert(wd),
                      const(wgs), const(wus), const(wds)],
            out_specs=tile,
            scratch_shapes=[pltpu.VMEM((t_rows + ROW_GROUP, d), F32), pltpu.VMEM((ch, d), F32),
                            pltpu.VMEM((ch, d), F32)]),
        out_shape=jax.ShapeDtypeStruct((r, d), F32),
        compiler_params=_cparams(("arbitrary", "arbitrary")), name="moe",
    )(cnt, off, tok, gs, h, x, gate5, wg, wu, wd, wgs, wus, wds)


def _dil_prompt_kernel(q_ref, k_ref, v_ref, tbl_ref, o_ref, qs_ref, m_ref, l_ref, acc_ref, *, tq, nd, n_pairs):
    i = pl.program_id(2)
    lo_half = lax.broadcasted_iota(I32, (tq, LANE), 1) < HEAD_DIM
    for hp in range(n_pairs):
        qs_ref[hp] = _stack_pair(q_ref[:, hp * LANE:(hp + 1) * LANE], lo_half)
    _flash_init(m_ref, l_ref, acc_ref)

    def attend(j, _):
        rows = pl.ds(pl.multiple_of(j * tq, tq), tq)
        b = tbl_ref[i - j]
        bias2 = jnp.concatenate([b, b], axis=0)
        for hp in range(n_pairs):
            cols = slice(hp * LANE, (hp + 1) * LANE)
            _flash_step(qs_ref[hp], k_ref[rows, cols], v_ref[rows, cols], bias2, m_ref, l_ref, acc_ref, hp)
        return 0

    lax.fori_loop(jnp.maximum(i - (nd - 1), 0), i + 1, attend, 0)
    for hp in range(n_pairs):
        o_ref[:, hp * LANE:(hp + 1) * LANE] = _flash_finish(l_ref, acc_ref, hp, lo_half, tq).astype(o_ref.dtype)


def _dilated_prompt(q, k, v):
    b, s, w = q.shape
    tq = min(256, s)
    nd = max(wd for wd, _ in DILATED_PATTERNS) // tq + 1
    dd = np.arange(nd)[:, None, None]
    rel = dd * tq + np.arange(tq)[None, :, None] - np.arange(tq)[None, None, :]
    tbl = jnp.asarray(_mult_bias(rel))
    n_pairs = 2
    gw = n_pairs * LANE
    state = pltpu.VMEM((n_pairs, 2 * tq, LANE), F32)
    return pl.pallas_call(
        functools.partial(_dil_prompt_kernel, tq=tq, nd=nd, n_pairs=n_pairs), grid=(b, w // gw, s // tq),
        in_specs=[pl.BlockSpec((None, tq, gw), lambda bi, hp, i: (bi, i, hp)),
                  pl.BlockSpec((None, s, gw), lambda bi, hp, i: (bi, 0, hp)),
                  pl.BlockSpec((None, s, gw), lambda bi, hp, i: (bi, 0, hp)),
                  pl.BlockSpec((nd, tq, tq), lambda bi, hp, i: (0, 0, 0))],
        out_specs=pl.BlockSpec((None, tq, gw), lambda bi, hp, i: (bi, i, hp)),
        out_shape=jax.ShapeDtypeStruct((b, s, w), BF16),
        scratch_shapes=[pltpu.VMEM((n_pairs, 2 * tq, LANE), BF16), state, state, state],
        compiler_params=_cparams(("arbitrary", "arbitrary", "arbitrary")), name="dilated_prompt",
    )(q, k, v, tbl)


def _dil_sample_kernel(qbd_ref, kc_ref, kn_ref, vc_ref, vn_ref, tc_ref, tn_ref, o_ref, *, n_new):
    qbd = qbd_ref[...]
    scale = HEAD_DIM ** -0.5
    sc = _dot(qbd, kc_ref[...].astype(BF16)) * scale + tc_ref[...]
    sn = _dot_nt(qbd, kn_ref[...].astype(BF16)) * scale + tn_ref[...]
    m = jnp.maximum(jnp.max(sc, axis=1, keepdims=True), jnp.max(sn, axis=1, keepdims=True))
    pc = jnp.exp(sc - m)
    pn = jnp.exp(sn - m)
    l = jnp.sum(pc, axis=1, keepdims=True) + jnp.sum(pn, axis=1, keepdims=True)
    acc = _dot_nt(pc.astype(BF16), vc_ref[...].astype(BF16)) + _dot(pn.astype(BF16), vn_ref[...].astype(BF16))
    o_ref[...] = _block_diag_rows(acc / l, N_HEADS_C, n_new)


def _dilated_sample(q, k_new, v_new, cache_k, cache_v):
    db, ds, w = q.shape
    wb = cache_k.shape[1]
    rows = N_HEADS_C * ds
    qh = q.reshape(db, ds, N_HEADS_C, HEAD_DIM).transpose(0, 2, 1, 3)
    eye = jnp.eye(N_HEADS_C, dtype=q.dtype)
    qbd = (qh[:, :, :, None, :] * eye[None, :, None, :, None]).reshape(db, rows, w)
    kt = jnp.transpose(cache_k, (0, 2, 3, 1)).reshape(db, w, wb)
    vt = jnp.transpose(cache_v, (0, 2, 3, 1)).reshape(db, w, wb)
    pad = lambda a: jnp.pad(a, ((0, 0), (0, LANE - ds), (0, 0)))
    rel = (wb + np.arange(ds)[:, None]) - np.arange(wb + LANE)[None, :]
    bias = _mult_bias(rel)
    bias[:, wb + ds:] = NEG
    tbl = np.tile(bias, (N_HEADS_C, 1))
    per_b = lambda *shape: pl.BlockSpec((None,) + shape, lambda b: (b,) + (0,) * len(shape))
    const = lambda a: pl.BlockSpec(a.shape, lambda b: (0,) * a.ndim)
    tc, tn = jnp.asarray(tbl[:, :wb]), jnp.asarray(tbl[:, wb:])
    return pl.pallas_call(
        functools.partial(_dil_sample_kernel, n_new=ds), grid=(db,),
        in_specs=[per_b(rows, w), per_b(w, wb), per_b(LANE, w), per_b(w, wb), per_b(LANE, w), const(tc), const(tn)],
        out_specs=per_b(ds, w),
        out_shape=jax.ShapeDtypeStruct((db, ds, w), F32),
        compiler_params=_cparams(("arbitrary",)), name="dilated_sample",
    )(qbd, kt, pad(k_new), vt, pad(v_new), tc, tn)


def kernel(x_prompt, x_sample, c_prompt, c_sample, cache_a_k, cache_a_v, cache_a_idx_k, state_conv, cache_c_k, cache_c_v, page_table, norm_mix_g, norm_ffn_g, w_ada, b_ada, w_in_even, w_out_even, a_q_norm_g, a_k_norm_g, b_conv_w, b_conv_b, b_ln_g, b_ln_b, w_in_odd, w_out_odd, c_q_norm_g, c_k_norm_g, w_router, b_router, w_gate_e, w_up_e, w_down_e, w_gate_s, w_up_s, w_down_s):
    bp, sp, d = x_prompt.shape
    bs, ss, _ = x_sample.shape
    depth = w_ada.shape[0]
    past = page_table.shape[1] * cache_a_k.shape[2]
    win_buf = cache_c_k.shape[2]
    rp, rs = bp * sp, bs * ss

    tab_p = _rope_tables(jnp.arange(sp, dtype=I32))
    tab_s = _rope_tables(jnp.tile(past + jnp.arange(ss, dtype=I32), bs))
    row = lambda a: a.reshape(1, -1)
    gain2 = lambda g: jnp.tile(g, 2).reshape(1, LANE)

    xp = x_prompt.reshape(rp, d)
    xs = x_sample.reshape(rs, d)
    c_all = jnp.concatenate([c_prompt, c_sample], axis=0)
    c_all = jnp.pad(c_all, ((0, -c_all.shape[0] % 8), (0, 0)))

    ak_p, av_p, ai_p, ak_s, av_s, ai_s, cv_p, cv_s = [], [], [], [], [], [], [], []
    ck_p, cw_p, ck_s, cw_s = [], [], [], []
    for layer in range(depth):
        mod = _adaln(c_all, w_ada[layer], b_ada[layer])
        mod_p = [m.reshape(bp, 1, d) for m in jnp.split(mod[:bp], 6, axis=-1)]
        mod_s = [jnp.repeat(m, ss, axis=0) for m in jnp.split(mod[bp:bp + bs], 6, axis=-1)]
        gmix, gffn = row(norm_mix_g[layer]), row(norm_ffn_g[layer])
        if layer % 2 == 0:
            e = layer // 2
            w_in = _perm_w_even(w_in_even[e])
            w_out = w_out_even[e].astype(BF16)
            qg, kg = gain2(a_q_norm_g[e]), gain2(a_k_norm_g[e])
            outs = [(A_W, BF16), (A_W, F32), (A_W, BF16), (A_W, F32), (A_W, BF16), (2 * LANE, BF16),
                    (LANE, F32), (LANE, BF16), (LANE, F32), (CONV_CH, F32)]
            q, k, kb, v, vb, qi, ki2, ki2b, wi, u = _project(
                _proj_even_kernel, xp, gmix, mod_p[0], mod_p[1], w_in, tab_p, qg, kg, sp, outs)
            b3 = lambda a: a.reshape(bp, sp, a.shape[-1])
            oa = _dsa_prompt(b3(q), b3(qi), b3(wi), b3(ki2b), b3(kb), b3(vb)).reshape(rp, A_W)
            u3 = b3(u)
            ob = _conv_module(u3, jnp.zeros((bp, CONV_WIDTH - 1, CONV_CH), F32),
                              b_conv_w[e], b_conv_b[e], b_ln_g[e], b_ln_b[e]).reshape(rp, CONV_CH)
            lhs_p = [oa, ob]
            ak_p.append(k.reshape(bp, sp, N_HEADS_A, HEAD_DIM))
            av_p.append(v.reshape(bp, sp, N_HEADS_A, HEAD_DIM))
            ai_p.append(ki2[:, :IDX_DIM].reshape(bp, sp, IDX_DIM))
            xpad = jnp.concatenate([jnp.zeros((bp, CONV_WIDTH - 1, CONV_CH), F32), u3], axis=1)
            cv_p.append(xpad[:, -(CONV_WIDTH - 1):])
            q, k, kb, v, vb, qi, ki2, ki2b, wi, u = _project(
                _proj_even_kernel, xs, gmix, mod_s[0], mod_s[1], w_in, tab_s, qg, kg, rs, outs)
            s3 = lambda a: a.reshape(bs, ss, a.shape[-1])
            oa = _dsa_sample(s3(q), s3(k), s3(v), s3(qi), s3(ki2)[:, :, :IDX_DIM], s3(wi)[:, :, :N_IDX_HEADS],
                             cache_a_k[e], cache_a_v[e], cache_a_idx_k[e], page_table)
            u3 = s3(u)
            ob = _conv_module(u3, state_conv[e], b_conv_w[e], b_conv_b[e], b_ln_g[e], b_ln_b[e])
            lhs_s = [oa.reshape(rs, A_W).astype(BF16), ob.reshape(rs, CONV_CH)]
            ak_s.append(k.reshape(bs, ss, N_HEADS_A, HEAD_DIM))
            av_s.append(v.reshape(bs, ss, N_HEADS_A, HEAD_DIM))
            ai_s.append(ki2[:, :IDX_DIM].reshape(bs, ss, IDX_DIM))
            cv_s.append(jnp.concatenate([state_conv[e], u3], axis=1)[:, -(CONV_WIDTH - 1):])
            ws = [w_out[:A_W], w_out[A_W:]]
        else:
            o = layer // 2
            w_in = w_in_odd[o].astype(BF16)
            qg, kg = gain2(c_q_norm_g[o]), gain2(c_k_norm_g[o])
            outs = [(C_W, BF16), (C_W, F32), (C_W, BF16), (C_W, F32), (C_W, BF16)]
            q, k, kb, v, vb = _project(_proj_odd_kernel, xp, gmix, mod_p[0], mod_p[1], w_in, tab_p, qg, kg, sp, outs)
            b3 = lambda a: a.reshape(bp, sp, a.shape[-1])
            lhs_p = [_dilated_prompt(b3(q), b3(kb), b3(vb)).reshape(rp, C_W)]
            k4, v4 = k.reshape(bp, sp, N_HEADS_C, HEAD_DIM), v.reshape(bp, sp, N_HEADS_C, HEAD_DIM)
            padt = max(win_buf - sp, 0)
            tail = lambda a: jnp.pad(a, ((0, 0), (padt, 0), (0, 0), (0, 0)))[:, -win_buf:]
            ck_p.append(tail(k4))
            cw_p.append(tail(v4))
            q, k, kb, v, vb = _project(_proj_odd_kernel, xs, gmix, mod_s[0], mod_s[1], w_in, tab_s, qg, kg, rs, outs)
            s3 = lambda a: a.reshape(bs, ss, a.shape[-1])
            att = _dilated_sample(s3(q), s3(k), s3(v), cache_c_k[o], cache_c_v[o])
            lhs_s = [att.reshape(rs, C_W).astype(BF16)]
            k4, v4 = k.reshape(bs, ss, N_HEADS_C, HEAD_DIM), v.reshape(bs, ss, N_HEADS_C, HEAD_DIM)
            ck_s.append(jnp.concatenate([cache_c_k[o], k4], axis=1)[:, -win_buf:])
            cw_s.append(jnp.concatenate([cache_c_v[o], v4], axis=1)[:, -win_buf:])
            ws = [w_out_odd[o].astype(BF16)]

        wg, wu, wd = w_gate_e[layer].astype(BF16), w_up_e[layer].astype(BF16), w_down_e[layer].astype(BF16)
        wgs, wus, wds = w_gate_s[layer].astype(BF16), w_up_s[layer].astype(BF16), w_down_s[layer].astype(BF16)
        xn, h, idx, gate = _outproj_router(lhs_p, ws, xp, mod_p[2], gffn, mod_p[3], mod_p[4],
                                           w_router[layer], b_router[layer], sp)
        xp = _moe(h, xn, mod_p[5], idx[:, :TOP_K], gate[:, :TOP_K], wg, wu, wd, wgs, wus, wds, sp)
        xn, h, idx, gate = _outproj_router(lhs_s, ws, xs, mod_s[2], gffn, mod_s[3], mod_s[4],
                                           w_router[layer], b_router[layer], rs)
        xs = _moe(h, xn, mod_s[5], idx[:, :TOP_K], gate[:, :TOP_K], wg, wu, wd, wgs, wus, wds, rs)

    return (xp.reshape(bp, sp, d), xs.reshape(bs, ss, d),
            jnp.stack(ak_p), jnp.stack(av_p), jnp.stack(ai_p),
            jnp.stack(ak_s), jnp.stack(av_s), jnp.stack(ai_s),
            jnp.stack(cv_p), jnp.stack(cv_s),
            jnp.stack(ck_p), jnp.stack(cw_p), jnp.stack(ck_s), jnp.stack(cw_s))
```

```python
import functools

import numpy as np
import jax
import jax.numpy as jnp
from jax import lax
from jax.experimental import pallas as pl
from jax.experimental.pallas import tpu as pltpu

F32, BF16, I32 = jnp.float32, jnp.bfloat16, jnp.int32

HEAD_DIM = 64
N_HEADS_A = 8
CONV_CH = 512
CONV_WIDTH = 31
N_IDX_HEADS = 4
IDX_DIM = 64
TOPK_MAX = 256
N_HEADS_C = 16
DILATED_PATTERNS = ((128, 1), (512, 4), (2048, 16))
N_EXPERTS = 64
TOP_K = 6
ROUTED_SCALE = 2.5
ROPE_THETA = 10000.0
EPS = 1e-6

LANE = 128
ROW_GROUP = 8
DSA_Q_TILE = 256
DIL_Q_TILE = 512
NEG = -1e30
KEY_NEG_INF = -2139095041
VMEM_LIMIT = 60 * 1024 * 1024

A_W = N_HEADS_A * HEAD_DIM
C_W = N_HEADS_C * HEAD_DIM


def _cparams(sem):
    return pltpu.CompilerParams(dimension_semantics=sem, vmem_limit_bytes=VMEM_LIMIT)


def _dot(a, b):
    return jnp.dot(a, b, preferred_element_type=F32)


def _dot_nt(a, b):
    return lax.dot_general(a, b, (((1,), (1,)), ((), ())), preferred_element_type=F32)


def _dot_tn(a, b):
    return lax.dot_general(a, b, (((0,), (0,)), ((), ())), preferred_element_type=F32)


def _silu(x):
    return x * jax.nn.sigmoid(x)


def _rms_mod(x, g, shift, scale):
    ms = jnp.mean(x * x, axis=-1, keepdims=True)
    y = x * lax.rsqrt(ms + EPS) * g
    return y * (1.0 + scale) + shift


def _to_key(x):
    b = lax.bitcast_convert_type(x, I32)
    k = jnp.where(b < 0, b ^ 0x7FFFFFFF, b)
    return jnp.where(k == -1, 0, k)


def _rope_tables(pos):
    half = HEAD_DIM // 2
    inv = ROPE_THETA ** (-jnp.arange(half, dtype=F32) / half)
    ang = pos.astype(F32)[:, None] * inv[None, :]
    c, s = jnp.cos(ang), jnp.sin(ang)
    z = jnp.zeros_like(s)
    cos = jnp.tile(c, (1, 4))
    sin_a = jnp.tile(jnp.concatenate([z, s], axis=1), (1, 2))
    sin_b = jnp.tile(jnp.concatenate([-s, z], axis=1), (1, 2))
    return cos, sin_a, sin_b


def _head_sum_matrix():
    r = np.arange(LANE)
    return jnp.asarray((r[:, None] // HEAD_DIM == r[None, :] // HEAD_DIM).astype(np.float32), BF16)


def _prefix_matrix(n):
    r = np.arange(n)
    return jnp.asarray((r[:, None] <= r[None, :]).astype(np.float32), BF16)


def _mult_bias(rel):
    mult = np.zeros(rel.shape, np.int64)
    for w, d in DILATED_PATTERNS:
        mult += ((rel >= 0) & (rel <= w) & (rel % d == 0))
    return np.where(mult > 0, np.log(np.maximum(mult, 1)), NEG).astype(np.float32)


def _ada_kernel(c_ref, w_ref, b_ref, o_ref):
    a = _silu(c_ref[...]).astype(BF16)
    o_ref[...] = _dot(a, w_ref[...].astype(BF16)) + b_ref[...]


def _adaln(c, w, b):
    m, d = c.shape
    n = w.shape[1]
    tn = 1536
    return pl.pallas_call(
        _ada_kernel, grid=(n // tn,),
        in_specs=[pl.BlockSpec((m, d), lambda j: (0, 0)),
                  pl.BlockSpec((d, tn), lambda j: (0, j)),
                  pl.BlockSpec((1, tn), lambda j: (0, j))],
        out_specs=pl.BlockSpec((m, tn), lambda j: (0, j)),
        out_shape=jax.ShapeDtypeStruct((m, n), F32),
        compiler_params=_cparams(("arbitrary",)), name="adaln")(c, w, b.reshape(1, n))


def _norm_rope(z, bsum, g, cos, sin_a, sin_b, norm):
    if norm:
        sq = z * z
        hi = sq.astype(BF16)
        lo = (sq - hi.astype(F32)).astype(BF16)
        ms = (_dot(hi, bsum) + _dot(lo, bsum)) * (1.0 / HEAD_DIM)
        z = z * lax.rsqrt(ms + EPS) * g
    return z * cos + pltpu.roll(z, 32, 1) * sin_a + pltpu.roll(z, 96, 1) * sin_b


def _proj_even_kernel(x_ref, gn_ref, sh_ref, sc_ref, w_ref, cos_ref, sa_ref, sb_ref, qg_ref, kg_ref, bsum_ref,
                      q_ref, k_ref, kb_ref, v_ref, vb_ref, qi_ref, ki_ref, kib_ref, wi_ref, u_ref, h_scr):
    h_scr[...] = _rms_mod(x_ref[...], gn_ref[...], sh_ref[...], sc_ref[...]).astype(BF16)
    hb = h_scr[...]
    cos, sa, sb, bsum = cos_ref[...], sa_ref[...], sb_ref[...], bsum_ref[...]

    def seg(lo, n):
        return _dot(hb, w_ref[:, lo:lo + n])

    z = seg(0, A_W)
    for c in range(A_W // LANE):
        sl = slice(c * LANE, (c + 1) * LANE)
        q_ref[:, sl] = _norm_rope(z[:, sl], bsum, qg_ref[...], cos, sa, sb, True).astype(BF16)
    z = seg(A_W, A_W)
    for c in range(A_W // LANE):
        sl = slice(c * LANE, (c + 1) * LANE)
        kk = _norm_rope(z[:, sl], bsum, kg_ref[...], cos, sa, sb, True)
        k_ref[:, sl] = kk
        kb_ref[:, sl] = kk.astype(BF16)
    z = seg(2 * A_W, A_W)
    v_ref[...] = z
    vb_ref[...] = z.astype(BF16)
    z = seg(3 * A_W, 2 * LANE)
    for c in range(2):
        sl = slice(c * LANE, (c + 1) * LANE)
        qi_ref[:, sl] = _norm_rope(z[:, sl], bsum, None, cos, sa, sb, False).astype(BF16)
    z = _norm_rope(seg(3 * A_W + 2 * LANE, LANE), bsum, None, cos, sa, sb, False)
    ki_ref[...] = z
    kib_ref[...] = z.astype(BF16)
    off = 3 * A_W + 3 * LANE
    ua = seg(off, CONV_CH)
    ub = seg(off + CONV_CH, CONV_CH)
    u_ref[...] = ua * jax.nn.sigmoid(ub)
    wi_ref[...] = seg(off + 2 * CONV_CH, LANE) * (N_IDX_HEADS ** -0.5 * IDX_DIM ** -0.5)


def _proj_odd_kernel(x_ref, gn_ref, sh_ref, sc_ref, w_ref, cos_ref, sa_ref, sb_ref, qg_ref, kg_ref, bsum_ref,
                     q_ref, k_ref, kb_ref, v_ref, vb_ref, h_scr):
    h_scr[...] = _rms_mod(x_ref[...], gn_ref[...], sh_ref[...], sc_ref[...]).astype(BF16)
    hb = h_scr[...]
    cos, sa, sb, bsum = cos_ref[...], sa_ref[...], sb_ref[...], bsum_ref[...]
    for half in range(2):
        z = _dot(hb, w_ref[:, half * A_W:(half + 1) * A_W])
        for c in range(A_W // LANE):
            sl = slice(c * LANE, (c + 1) * LANE)
            dl = slice(half * A_W + c * LANE, half * A_W + (c + 1) * LANE)
            q_ref[:, dl] = _norm_rope(z[:, sl], bsum, qg_ref[...], cos, sa, sb, True).astype(BF16)
    for half in range(2):
        z = _dot(hb, w_ref[:, C_W + half * A_W:C_W + (half + 1) * A_W])
        for c in range(A_W // LANE):
            sl = slice(c * LANE, (c + 1) * LANE)
            dl = slice(half * A_W + c * LANE, half * A_W + (c + 1) * LANE)
            kk = _norm_rope(z[:, sl], bsum, kg_ref[...], cos, sa, sb, True)
            k_ref[:, dl] = kk
            kb_ref[:, dl] = kk.astype(BF16)
    for half in range(2):
        z = _dot(hb, w_ref[:, 2 * C_W + half * A_W:2 * C_W + (half + 1) * A_W])
        v_ref[:, half * A_W:(half + 1) * A_W] = z
        vb_ref[:, half * A_W:(half + 1) * A_W] = z.astype(BF16)


def _mod_spec(mod, tr, tiles_per_group):
    d = mod.shape[-1]
    if mod.ndim == 3:
        return pl.BlockSpec((None, 1, d), lambda i: (i // tiles_per_group, 0, 0))
    return pl.BlockSpec((tr, d), lambda i: (i, 0))


def _row_tile(rows, want):
    return want if rows % want == 0 else rows


def _project(kern, x, gn, shift, scale, w, tables, qg, kg, group_rows, outs):
    r, d = x.shape
    tr = _row_tile(min(r, group_rows), 512)
    tpg = max(group_rows // tr, 1)
    tpt = tables[0].shape[0] // tr
    const = lambda a: pl.BlockSpec(a.shape, lambda i: (0,) * a.ndim)
    tab = pl.BlockSpec((tr, LANE), lambda i: (i % tpt, 0))
    bsum = _head_sum_matrix()
    return pl.pallas_call(
        kern, grid=(r // tr,),
        in_specs=[pl.BlockSpec((tr, d), lambda i: (i, 0)), const(gn), _mod_spec(shift, tr, tpg),
                  _mod_spec(scale, tr, tpg), const(w), tab, tab, tab, const(qg), const(kg), const(bsum)],
        out_specs=[pl.BlockSpec((tr, wd), lambda i: (i, 0)) for wd, _ in outs],
        out_shape=[jax.ShapeDtypeStruct((r, wd), dt) for wd, dt in outs],
        scratch_shapes=[pltpu.VMEM((tr, d), BF16)],
        compiler_params=_cparams(("arbitrary",)), name=kern.__name__.strip("_"),
    )(x, gn, shift, scale, w, *tables, qg, kg, bsum)


def _perm_w_even(w):
    d = w.shape[0]
    cuts = np.cumsum([A_W, A_W, A_W, N_IDX_HEADS * IDX_DIM, IDX_DIM, N_IDX_HEADS, CONV_CH])
    q, k, v, qi, ki, wi, ua, ub = jnp.split(w, [int(c) for c in cuts], axis=1)
    pad = jnp.zeros((d, LANE - N_IDX_HEADS), w.dtype)
    return jnp.concatenate([q, k, v, qi, ki, ki, ua, ub, wi, pad], axis=1).astype(BF16)


def _dsa_prompt_kernel(q_ref, qi_ref, wi_ref, ki_ref, k_ref, v_ref, u_ref, o_ref, key_ref, bias_ref,
                       qs_ref, m_ref, l_ref, acc_ref, *, tq, kc, topk):
    i = pl.program_id(1)
    q0 = i * tq
    cd = q0 // kc
    lane = lax.broadcasted_iota(I32, (tq, LANE), 1)
    lo_half = lane < HEAD_DIM
    zero = jnp.zeros((tq, LANE), BF16)

    qi = qi_ref[...]
    qm = [jnp.where(lo_half if h % 2 == 0 else ~lo_half, qi[:, (h // 2) * LANE:(h // 2 + 1) * LANE], zero)
          for h in range(N_IDX_HEADS)]
    wi = wi_ref[...]
    wcol = [wi[:, h:h + 1] for h in range(N_IDX_HEADS)]

    def score_chunk(c):
        kk = ki_ref[pl.ds(pl.multiple_of(c * kc, kc), kc), :]
        acc = None
        for h in range(N_IDX_HEADS):
            t = wcol[h] * jnp.maximum(_dot_nt(qm[h], kk), 0.0)
            acc = t if acc is None else acc + t
        return acc

    def fill(c, _):
        key_ref[c] = _to_key(score_chunk(c))
        return 0

    lax.fori_loop(0, cd, fill, 0)
    kpos = cd * kc + lax.broadcasted_iota(I32, (tq, kc), 1)
    qpos = q0 + lax.broadcasted_iota(I32, (tq, kc), 0)
    key_ref[cd] = _to_key(jnp.where(kpos <= qpos, score_chunk(cd), -jnp.inf))

    def count(cand, strict):
        def body(c, cnt):
            kk = key_ref[c]
            one = jnp.where((kk > cand) if strict else (kk >= cand), 1.0, 0.0)
            part = one[:, 0:LANE]
            for j in range(1, kc // LANE):
                part = part + one[:, j * LANE:(j + 1) * LANE]
            return cnt + part
        cnt = lax.fori_loop(0, cd + 1, body, jnp.zeros((tq, LANE), F32))
        return jnp.sum(cnt, axis=1, keepdims=True)

    kf = float(topk)
    thr = jnp.where(count(jnp.zeros((tq, 1), I32), False) >= kf, 0, -2 ** 31).astype(I32)

    def bit(b, thr):
        cand = thr + jnp.left_shift(jnp.int32(1), 30 - b)
        return jnp.where(count(cand, False) >= kf, cand, thr)

    thr = lax.fori_loop(0, 31, bit, thr)
    need = kf - count(thr, True)
    upper = u_ref[...]

    def select(c, run):
        kk = key_ref[c]
        tie = kk == thr
        pre = _dot(jnp.where(tie, 1.0, 0.0).astype(BF16), upper) + run
        sel = ((kk > thr) | (tie & (pre <= need))) & (kk > KEY_NEG_INF)
        bias_ref[c] = jnp.where(sel, 0.0, NEG)
        return pre[:, kc - 1:kc]

    lax.fori_loop(0, cd + 1, select, jnp.zeros((tq, 1), F32))

    n_pairs = A_W // LANE
    for hp in range(n_pairs):
        qs_ref[hp] = _stack_pair(q_ref[:, hp * LANE:(hp + 1) * LANE], lo_half)
    _flash_init(m_ref, l_ref, acc_ref)

    def attend(c, _):
        rows = pl.ds(pl.multiple_of(c * kc, kc), kc)
        b = bias_ref[c]
        bias2 = jnp.concatenate([b, b], axis=0)
        for hp in range(n_pairs):
            cols = slice(hp * LANE, (hp + 1) * LANE)
            _flash_step(qs_ref[hp], k_ref[rows, cols], v_ref[rows, cols], bias2, m_ref, l_ref, acc_ref, hp)
        return 0

    lax.fori_loop(0, cd + 1, attend, 0)
    for hp in range(n_pairs):
        o_ref[:, hp * LANE:(hp + 1) * LANE] = _flash_finish(l_ref, acc_ref, hp, lo_half, tq).astype(o_ref.dtype)


def _stack_pair(qp, lo_half):
    qsc = (qp.astype(F32) * (HEAD_DIM ** -0.5)).astype(BF16)
    zero = jnp.zeros_like(qsc)
    return jnp.concatenate([jnp.where(lo_half, qsc, zero), jnp.where(lo_half, zero, qsc)], axis=0)


def _flash_init(m_ref, l_ref, acc_ref):
    m_ref[...] = jnp.full(m_ref.shape, NEG, F32)
    l_ref[...] = jnp.zeros(l_ref.shape, F32)
    acc_ref[...] = jnp.zeros(acc_ref.shape, F32)


def _flash_step(qs, kblk, vblk, bias2, m_ref, l_ref, acc_ref, idx):
    s = _dot_nt(qs, kblk) + bias2
    m_old = m_ref[idx]
    mn = jnp.maximum(m_old, jnp.max(s, axis=1, keepdims=True))
    p = jnp.exp(s - jnp.tile(mn, (1, s.shape[1] // LANE)))
    a = jnp.exp(m_old - mn)
    l_ref[idx] = a * l_ref[idx] + jnp.sum(p, axis=1, keepdims=True)
    acc_ref[idx] = a * acc_ref[idx] + _dot(p.astype(BF16), vblk)
    m_ref[idx] = mn


def _flash_finish(l_ref, acc_ref, idx, lo_half, t):
    o = acc_ref[idx] / l_ref[idx]
    return jnp.where(lo_half, o[:t], o[t:])


def _dsa_prompt(q, qi, wi, ki2, k, v):
    b, s, _ = q.shape
    tq = min(DSA_Q_TILE, s)
    kc = min(512, s)
    topk = min(TOPK_MAX, s // 4)
    nc = s // kc
    upper = _prefix_matrix(kc)
    qspec = lambda w: pl.BlockSpec((None, tq, w), lambda bi, i: (bi, i, 0))
    kspec = lambda w: pl.BlockSpec((None, s, w), lambda bi, i: (bi, 0, 0), pipeline_mode=pl.Buffered(1))
    return pl.pallas_call(
        functools.partial(_dsa_prompt_kernel, tq=tq, kc=kc, topk=topk),
        grid=(b, s // tq),
        in_specs=[qspec(A_W), qspec(2 * LANE), qspec(LANE), kspec(LANE), kspec(A_W), kspec(A_W),
                  pl.BlockSpec((kc, kc), lambda bi, i: (0, 0))],
        out_specs=qspec(A_W),
        out_shape=jax.ShapeDtypeStruct((b, s, A_W), BF16),
        scratch_shapes=[pltpu.VMEM((nc, tq, kc), I32), pltpu.VMEM((nc, tq, kc), F32),
                        pltpu.VMEM((A_W // LANE, 2 * tq, LANE), BF16)]
        + [pltpu.VMEM((A_W // LANE, 2 * tq, LANE), F32)] * 3,
        compiler_params=_cparams(("arbitrary", "arbitrary")), name="dsa_prompt",
    )(q, qi, wi, ki2, k, v, upper)


def _dsa_sample_select_kernel(pt_ref, qi_ref, wi_ref, *rest, n_new, npg, pg, topk):
    del pt_ref
    kip_refs = rest[:pg]
    kin_ref, u_ref, o_ref, key_ref = rest[pg:]
    p = pl.program_id(1)
    qi = qi_ref[...]
    wi = wi_ref[...]
    n_chunks = key_ref.shape[0]

    def score(dots):
        r = jnp.maximum(dots, 0.0) * wi
        acc = r[0:n_new]
        for h in range(1, N_IDX_HEADS):
            acc = acc + r[h * n_new:(h + 1) * n_new]
        return acc

    for g in range(pg):
        key_ref[p * pg + g] = _to_key(score(_dot(qi, kip_refs[g][...].astype(BF16))))

    @pl.when(p == 0)
    def _():
        for c in range(npg + 1, n_chunks):
            key_ref[c] = jnp.full((n_new, LANE), -2 ** 31, I32)

    @pl.when(p == npg // pg - 1)
    def _():
        col = lax.broadcasted_iota(I32, (n_new, LANE), 1)
        row = lax.broadcasted_iota(I32, (n_new, LANE), 0)
        new = score(_dot_nt(qi, kin_ref[...].astype(BF16)))
        key_ref[npg] = _to_key(jnp.where(col <= row, new, -jnp.inf))

        def count(cand, strict):
            def body(c, cnt):
                kk = key_ref[pl.ds(pl.multiple_of(c * ROW_GROUP, ROW_GROUP), ROW_GROUP)]
                hit = jnp.where((kk > cand) if strict else (kk >= cand), 1.0, 0.0)
                return cnt + jnp.sum(hit, axis=0)
            cnt = lax.fori_loop(0, n_chunks // ROW_GROUP, body, jnp.zeros((n_new, LANE), F32))
            return jnp.sum(cnt, axis=1, keepdims=True)

        kf = float(topk)
        thr = jnp.where(count(jnp.zeros((n_new, 1), I32), False) >= kf, 0, -2 ** 31).astype(I32)

        def bit(b, thr):
            cand = thr + jnp.left_shift(jnp.int32(1), 30 - b)
            return jnp.where(count(cand, False) >= kf, cand, thr)

        thr = lax.fori_loop(0, 31, bit, thr)
        need = kf - count(thr, True)
        upper = u_ref[...]

        def select(g, run):
            c0 = pl.multiple_of(g * ROW_GROUP, ROW_GROUP)
            kk = key_ref[pl.ds(c0, ROW_GROUP)]
            tie = kk == thr
            ties = jnp.where(tie, 1.0, 0.0).reshape(ROW_GROUP * n_new, LANE).astype(BF16)
            pre = _dot(ties, upper).reshape(ROW_GROUP, n_new, LANE)
            for c in range(ROW_GROUP):
                pc = pre[c] + run
                sel = ((kk[c] > thr) | (tie[c] & (pc <= need))) & (kk[c] > KEY_NEG_INF)
                o_ref[c0 + c] = jnp.where(sel, 1.0, 0.0)
                run = pc[:, LANE - 1:LANE]
            return run

        lax.fori_loop(0, n_chunks // ROW_GROUP, select, jnp.zeros((n_new, 1), F32))


def _block_diag_rows(o_full, n_heads, n_new):
    head_of_lane = lax.broadcasted_iota(I32, (n_new, o_full.shape[1]), 1) // HEAD_DIM
    out = jnp.zeros((n_new, o_full.shape[1]), F32)
    for h in range(n_heads):
        out = jnp.where(head_of_lane == h, o_full[h * n_new:(h + 1) * n_new], out)
    return out


def _dsa_sample_attend_kernel(pt_ref, qbd_ref, *rest, npg, pg, n_new):
    del pt_ref
    kp_refs, vp_refs = rest[:pg], rest[pg:2 * pg]
    kn_ref, vn_ref, sel_ref, o_ref, m_ref, l_ref, acc_ref = rest[2 * pg:]
    p = pl.program_id(1)
    qbd = qbd_ref[...]

    @pl.when(p == 0)
    def _():
        _flash_init(m_ref, l_ref, acc_ref)

    def bias(sel):
        return (jnp.concatenate([sel] * N_HEADS_A, axis=0) - 1.0) * (-NEG)

    def update(scores, pv):
        m = m_ref[...]
        mn = m
        for s in scores:
            mn = jnp.maximum(mn, jnp.max(s, axis=1, keepdims=True))
        a = jnp.exp(m - mn)
        l = a * l_ref[...]
        acc = jnp.tile(a, (1, A_W // LANE)) * acc_ref[...]
        for s, f in zip(scores, pv):
            pr = jnp.exp(s - mn)
            l = l + jnp.sum(pr, axis=1, keepdims=True)
            acc = acc + f(pr.astype(BF16))
        m_ref[...] = mn
        l_ref[...] = l
        acc_ref[...] = acc

    scale = HEAD_DIM ** -0.5
    update([_dot(qbd, kp_refs[g][...].astype(BF16)) * scale + bias(sel_ref[p * pg + g]) for g in range(pg)],
           [lambda pr, g=g: _dot_nt(pr, vp_refs[g][...].astype(BF16)) for g in range(pg)])

    @pl.when(p == npg // pg - 1)
    def _():
        update([_dot_nt(qbd, kn_ref[...].astype(BF16)) * scale + bias(sel_ref[npg])],
               [lambda pr: _dot(pr, vn_ref[...].astype(BF16))])
        o_full = acc_ref[...] / jnp.tile(l_ref[...], (1, A_W // LANE))
        o_ref[...] = _block_diag_rows(o_full, N_HEADS_A, n_new)


def _dsa_sample(q, k_new, v_new, qi, ki_new, wi, pool_k, pool_v, pool_ki, page_table):
    db, ds, _ = q.shape
    n_phys, page = pool_k.shape[0], pool_k.shape[1]
    npg = page_table.shape[1]
    topk = min(TOPK_MAX, (npg * page + ds) // 4)
    pt = page_table.reshape(-1).astype(I32)
    hq = N_IDX_HEADS * ds
    qi_r = qi.reshape(db, ds, N_IDX_HEADS, IDX_DIM).transpose(0, 2, 1, 3).reshape(db, hq, IDX_DIM)
    wi_r = jnp.broadcast_to(wi.transpose(0, 2, 1).reshape(db, hq, 1), (db, hq, LANE))
    pad = lambda a: jnp.pad(a, ((0, 0), (0, page - ds), (0, 0)))
    per_b = lambda *shape: pl.BlockSpec((None,) + shape, lambda b, p, pt: (b,) + (0,) * len(shape))
    pg = next(g for g in (8, 4, 2, 1) if npg % g == 0)
    paged = lambda w: [pl.BlockSpec((None, w, page), lambda b, p, pt, g=g: (pt[b * npg + p * pg + g], 0, 0))
                       for g in range(pg)]
    pool_kt = jnp.transpose(pool_k, (0, 2, 3, 1)).reshape(n_phys, A_W, page)
    pool_vt = jnp.transpose(pool_v, (0, 2, 3, 1)).reshape(n_phys, A_W, page)
    pool_kit = jnp.transpose(pool_ki, (0, 2, 1))
    n_chunks = -(-(npg + 1) // ROW_GROUP) * ROW_GROUP
    sel = pl.pallas_call(
        functools.partial(_dsa_sample_select_kernel, n_new=ds, npg=npg, pg=pg, topk=topk),
        grid_spec=pltpu.PrefetchScalarGridSpec(
            num_scalar_prefetch=1, grid=(db, npg // pg),
            in_specs=[per_b(hq, IDX_DIM), per_b(hq, LANE)] + paged(IDX_DIM)
            + [per_b(page, IDX_DIM), pl.BlockSpec((page, page), lambda b, p, pt: (0, 0))],
            out_specs=per_b(n_chunks, ds, page),
            scratch_shapes=[pltpu.VMEM((n_chunks, ds, page), I32)]),
        out_shape=jax.ShapeDtypeStruct((db, n_chunks, ds, page), F32),
        compiler_params=_cparams(("arbitrary", "arbitrary")), name="dsa_sample_select",
    )(pt, qi_r, wi_r, *([pool_kit] * pg), pad(ki_new), _prefix_matrix(page))

    rows = N_HEADS_A * ds
    qh = q.reshape(db, ds, N_HEADS_A, HEAD_DIM).transpose(0, 2, 1, 3)
    eye = jnp.eye(N_HEADS_A, dtype=q.dtype)
    qbd = (qh[:, :, :, None, :] * eye[None, :, None, :, None]).reshape(db, rows, A_W)
    state = pltpu.VMEM((rows, LANE), F32)
    return pl.pallas_call(
        functools.partial(_dsa_sample_attend_kernel, npg=npg, pg=pg, n_new=ds),
        grid_spec=pltpu.PrefetchScalarGridSpec(
            num_scalar_prefetch=1, grid=(db, npg // pg),
            in_specs=[per_b(rows, A_W)] + paged(A_W) + paged(A_W)
            + [per_b(page, A_W), per_b(page, A_W), per_b(n_chunks, ds, page)],
            out_specs=per_b(ds, A_W),
            scratch_shapes=[state, state, pltpu.VMEM((rows, A_W), F32)]),
        out_shape=jax.ShapeDtypeStruct((db, ds, A_W), F32),
        compiler_params=_cparams(("arbitrary", "arbitrary")), name="dsa_sample_attend",
    )(pt, qbd, *([pool_kt] * pg), *([pool_vt] * pg), pad(k_new), pad(v_new), sel)


def _conv_kernel(u_ref, halo_ref, buf_ref, w_ref, b_ref, g_ref, bb_ref, o_ref, xp_ref, *, tt):
    i = pl.program_id(1)

    @pl.when(i == 0)
    def _():
        xp_ref[0:32, :] = buf_ref[...]

    @pl.when(i > 0)
    def _():
        xp_ref[0:32, :] = halo_ref[...]

    xp_ref[32:32 + tt, :] = u_ref[...]
    w = w_ref[...]
    acc = jnp.zeros((tt, CONV_CH), F32) + b_ref[...]
    for j in range(CONV_WIDTH):
        acc = acc + xp_ref[2 + j:2 + j + tt, :] * w[j:j + 1, :]
    mu = jnp.mean(acc, axis=-1, keepdims=True)
    var = jnp.mean(jnp.square(acc - mu), axis=-1, keepdims=True)
    yn = (acc - mu) * lax.rsqrt(var + EPS) * g_ref[...] + bb_ref[...]
    o_ref[...] = _silu(yn).astype(o_ref.dtype)


def _conv_module(u, buf30, conv_w, conv_b, ln_g, ln_b):
    b, t, c = u.shape
    tt = _row_tile(t, 512)
    buf = jnp.pad(buf30, ((0, 0), (2, 0), (0, 0)))
    halo_src = u if t >= 32 else buf
    hb = tt // 32
    w = jnp.pad(conv_w, ((0, 32 - CONV_WIDTH), (0, 0)))
    const = lambda a: pl.BlockSpec(a.shape, lambda bi, i: (0,) * a.ndim)
    row = lambda a: a.reshape(1, c)
    return pl.pallas_call(
        functools.partial(_conv_kernel, tt=tt), grid=(b, t // tt),
        in_specs=[pl.BlockSpec((None, tt, c), lambda bi, i: (bi, i, 0)),
                  pl.BlockSpec((None, 32, c), lambda bi, i: (bi, jnp.maximum(i * hb - 1, 0), 0)),
                  pl.BlockSpec((None, 32, c), lambda bi, i: (bi, 0, 0)),
                  const(w), const(row(conv_b)), const(row(ln_g)), const(row(ln_b))],
        out_specs=pl.BlockSpec((None, tt, c), lambda bi, i: (bi, i, 0)),
        out_shape=jax.ShapeDtypeStruct((b, t, c), BF16),
        scratch_shapes=[pltpu.VMEM((32 + tt, c), F32)],
        compiler_params=_cparams(("arbitrary", "arbitrary")), name="conv_module",
    )(u, halo_src, buf, w, row(conv_b), row(ln_g), row(ln_b))


def _outproj_kernel(*refs, n_lhs):
    lhs, ws = refs[:n_lhs], refs[n_lhs:2 * n_lhs]
    x_ref, g2_ref, gn_ref, sh_ref, sc_ref, wr_ref, br_ref, xn_ref, h_ref, idx_ref, gate_ref = refs[2 * n_lhs:]
    mix = _dot(lhs[0][...], ws[0][...])
    for a, w in zip(lhs[1:], ws[1:]):
        mix = mix + _dot(a[...], w[...])
    xn = x_ref[...] + g2_ref[...] * mix
    xn_ref[...] = xn
    h = _rms_mod(xn, gn_ref[...], sh_ref[...], sc_ref[...])
    h_ref[...] = h
    s = jax.nn.sigmoid(_dot(h.astype(BF16), wr_ref[...]))
    tr = s.shape[0]
    lane = lax.broadcasted_iota(I32, (tr, LANE), 1)
    lanef = lane.astype(F32)
    vals = jnp.where(lane < N_EXPERTS, s + br_ref[...], -jnp.inf)
    idxm = jnp.zeros((tr, LANE), F32)
    gm = jnp.zeros((tr, LANE), F32)
    for j in range(TOP_K):
        m = jnp.max(vals, axis=1, keepdims=True)
        ix = jnp.min(jnp.where(vals == m, lanef, float(LANE)), axis=1, keepdims=True)
        hit = lanef == ix
        sj = jnp.sum(jnp.where(hit, s, 0.0), axis=1, keepdims=True)
        idxm = jnp.where(lane == j, ix, idxm)
        gm = jnp.where(lane == j, sj, gm)
        vals = jnp.where(hit, -jnp.inf, vals)
    idx_ref[...] = idxm.astype(I32)
    gate_ref[...] = gm / jnp.sum(gm, axis=1, keepdims=True) * ROUTED_SCALE


def _outproj_router(lhs, ws, x, gate2, gn, shift, scale, w_router, b_router, group_rows):
    r, d = x.shape
    tr = _row_tile(min(r, group_rows), 512)
    tpg = max(group_rows // tr, 1)
    const = lambda a: pl.BlockSpec(a.shape, lambda i: (0,) * a.ndim)
    rowblk = lambda w: pl.BlockSpec((tr, w), lambda i: (i, 0))
    wr = jnp.pad(w_router, ((0, 0), (0, LANE - N_EXPERTS))).astype(BF16)
    br = jnp.pad(b_router, (0, LANE - N_EXPERTS)).reshape(1, LANE)
    return pl.pallas_call(
        functools.partial(_outproj_kernel, n_lhs=len(lhs)), grid=(r // tr,),
        in_specs=[rowblk(a.shape[1]) for a in lhs] + [const(w) for w in ws]
        + [rowblk(d), _mod_spec(gate2, tr, tpg), const(gn), _mod_spec(shift, tr, tpg), _mod_spec(scale, tr, tpg),
           const(wr), const(br)],
        out_specs=[rowblk(d), rowblk(d), rowblk(LANE), rowblk(LANE)],
        out_shape=[jax.ShapeDtypeStruct((r, d), F32), jax.ShapeDtypeStruct((r, d), F32),
                   jax.ShapeDtypeStruct((r, LANE), I32), jax.ShapeDtypeStruct((r, LANE), F32)],
        compiler_params=_cparams(("arbitrary",)), name="outproj_router",
    )(*lhs, *ws, x, gate2, gn, shift, scale, wr, br)


def _moe_kernel(cnt_ref, off_ref, tok_ref, gate_ref, h_ref, x_ref, g5_ref, wg_ref, wu_ref, wd_ref,
                wgs_ref, wus_ref, wds_ref, o_ref, acc_ref, xg_ref, y_ref, *, t_rows, ch):
    t = pl.program_id(0)
    e = pl.program_id(1)
    ne = pl.num_programs(1)

    def ffn(xb, wg, wu, wd):
        a = _silu(_dot(xb, wg)) * _dot(xb, wu)
        return _dot(a.astype(BF16), wd)

    @pl.when(e == 0)
    def _():
        xg_ref[...] = jnp.zeros(xg_ref.shape, F32)
        acc_ref[t_rows:t_rows + ROW_GROUP, :] = jnp.zeros((ROW_GROUP, acc_ref.shape[1]), F32)

        def shared(c, _):
            rows = pl.ds(pl.multiple_of(c * ch, ch), ch)
            acc_ref[rows, :] = ffn(h_ref[rows, :].astype(BF16), wgs_ref[...], wus_ref[...], wds_ref[...])
            return 0

        lax.fori_loop(0, t_rows // ch, shared, 0)

    n = cnt_ref[t * ne + e]
    base = off_ref[t * ne + e]

    def chunk(c, _):
        r0 = base + c * ch
        rows = jnp.minimum(ch, n - c * ch)
        groups = lax.shift_right_logical(rows + (ROW_GROUP - 1), ROW_GROUP.bit_length() - 1)

        def gather(g, _):
            r = r0 + g * ROW_GROUP
            picked = [h_ref[pl.ds(tok_ref[r + j], 1), :] for j in range(ROW_GROUP)]
            xg_ref[pl.ds(pl.multiple_of(g * ROW_GROUP, ROW_GROUP), ROW_GROUP), :] = jnp.concatenate(picked, axis=0)
            return 0

        lax.fori_loop(0, groups, gather, 0)
        y_ref[...] = ffn(xg_ref[...].astype(BF16), wg_ref[...], wu_ref[...], wd_ref[...])

        def scatter(g, _):
            y8 = y_ref[pl.ds(pl.multiple_of(g * ROW_GROUP, ROW_GROUP), ROW_GROUP), :]
            dst, gate = [], []
            for j in range(ROW_GROUP):
                r = g * ROW_GROUP + j
                ok = r < rows
                dst.append(jnp.where(ok, tok_ref[r0 + r], t_rows + j))
                gate.append(jnp.where(ok, gate_ref[r0 + r], 0.0))
            new = [acc_ref[pl.ds(dst[j], 1), :] + gate[j] * y8[j:j + 1, :] for j in range(ROW_GROUP)]
            for j in range(ROW_GROUP):
                acc_ref[pl.ds(dst[j], 1), :] = new[j]
            return 0

        lax.fori_loop(0, groups, scatter, 0)
        return 0

    lax.fori_loop(0, (n + ch - 1) // ch, chunk, 0)

    @pl.when(e == ne - 1)
    def _():
        o_ref[...] = x_ref[...] + g5_ref[...] * acc_ref[0:t_rows, :]


def _moe(h, x, gate5, idx6, g6, wg, wu, wd, wgs, wus, wds, group_rows):
    r, d = h.shape
    t_rows = _row_tile(min(r, group_rows), 1024)
    ch = min(128, t_rows)
    nt = r // t_rows
    per = t_rows * TOP_K
    tpg = max(group_rows // t_rows, 1)
    e = idx6.reshape(nt, per)
    order = jnp.argsort(e, axis=-1, stable=True)
    per_pad = -(-(per + ROW_GROUP) // 1024) * 1024
    padl = lambda a: jnp.pad(a, ((0, 0), (0, per_pad - per))).reshape(-1)
    tok = padl((order // TOP_K).astype(I32))
    gs = padl(jnp.take_along_axis(g6.reshape(nt, per), order, axis=-1))
    cnt = jnp.sum((e[:, :, None] == jnp.arange(N_EXPERTS, dtype=I32)).astype(I32), axis=1)
    off = (jnp.cumsum(cnt, axis=-1) - cnt).astype(I32).reshape(-1)
    cnt = cnt.reshape(-1)
    if gate5.ndim == 3:
        g5spec = pl.BlockSpec((None, 1, d), lambda ti, ei, *_: (ti // tpg, 0, 0))
    else:
        g5spec = pl.BlockSpec((t_rows, d), lambda ti, ei, *_: (ti, 0))
    tile = pl.BlockSpec((t_rows, d), lambda ti, ei, *_: (ti, 0))
    smem = pl.BlockSpec((per_pad,), lambda ti, ei, *_: (ti,), memory_space=pltpu.SMEM)
    expert = lambda a: pl.BlockSpec((None,) + a.shape[1:], lambda ti, ei, *_: (ei, 0, 0))
    const = lambda a: pl.BlockSpec(a.shape, lambda ti, ei, *_: (0,) * a.ndim)
    return pl.pallas_call(
        functools.partial(_moe_kernel, t_rows=t_rows, ch=ch),
        grid_spec=pltpu.PrefetchScalarGridSpec(
            num_scalar_prefetch=2, grid=(nt, N_EXPERTS),
            in_specs=[smem, smem, tile, tile, g5spec, expert(wg), expert(wu), expert(wd),
                      const(wgs), const(wus), const(wds)],
            out_specs=tile,
            scratch_shapes=[pltpu.VMEM((t_rows + ROW_GROUP, d), F32), pltpu.VMEM((ch, d), F32),
                            pltpu.VMEM((ch, d), F32)]),
        out_shape=jax.ShapeDtypeStruct((r, d), F32),
        compiler_params=_cparams(("arbitrary", "arbitrary")), name="moe",
    )(cnt, off, tok, gs, h, x, gate5, wg, wu, wd, wgs, wus, wds)


def _dil_prompt_kernel(q_ref, k_ref, v_ref, tbl_ref, o_ref, qs_ref, m_ref, l_ref, acc_ref, *, tq, nd, n_pairs):
    i = pl.program_id(2)
    lo_half = lax.broadcasted_iota(I32, (tq, LANE), 1) < HEAD_DIM
    for hp in range(n_pairs):
        qs_ref[hp] = _stack_pair(q_ref[:, hp * LANE:(hp + 1) * LANE], lo_half)
    _flash_init(m_ref, l_ref, acc_ref)

    def attend(j, _):
        rows = pl.ds(pl.multiple_of(j * tq, tq), tq)
        b = tbl_ref[i - j]
        bias2 = jnp.concatenate([b, b], axis=0)
        for hp in range(n_pairs):
            cols = slice(hp * LANE, (hp + 1) * LANE)
            _flash_step(qs_ref[hp], k_ref[rows, cols], v_ref[rows, cols], bias2, m_ref, l_ref, acc_ref, hp)
        return 0

    lax.fori_loop(jnp.maximum(i - (nd - 1), 0), i + 1, attend, 0)
    for hp in range(n_pairs):
        o_ref[:, hp * LANE:(hp + 1) * LANE] = _flash_finish(l_ref, acc_ref, hp, lo_half, tq).astype(o_ref.dtype)


def _dilated_prompt(q, k, v):
    b, s, w = q.shape
    tq = min(DIL_Q_TILE, s)
    nd = max(wd for wd, _ in DILATED_PATTERNS) // tq + 1
    dd = np.arange(nd)[:, None, None]
    rel = dd * tq + np.arange(tq)[None, :, None] - np.arange(tq)[None, None, :]
    tbl = jnp.asarray(_mult_bias(rel))
    n_pairs = 2
    gw = n_pairs * LANE
    state = pltpu.VMEM((n_pairs, 2 * tq, LANE), F32)
    return pl.pallas_call(
        functools.partial(_dil_prompt_kernel, tq=tq, nd=nd, n_pairs=n_pairs), grid=(b, w // gw, s // tq),
        in_specs=[pl.BlockSpec((None, tq, gw), lambda bi, hp, i: (bi, i, hp)),
                  pl.BlockSpec((None, s, gw), lambda bi, hp, i: (bi, 0, hp)),
                  pl.BlockSpec((None, s, gw), lambda bi, hp, i: (bi, 0, hp)),
                  pl.BlockSpec((nd, tq, tq), lambda bi, hp, i: (0, 0, 0))],
        out_specs=pl.BlockSpec((None, tq, gw), lambda bi, hp, i: (bi, i, hp)),
        out_shape=jax.ShapeDtypeStruct((b, s, w), BF16),
        scratch_shapes=[pltpu.VMEM((n_pairs, 2 * tq, LANE), BF16), state, state, state],
        compiler_params=_cparams(("arbitrary", "arbitrary", "arbitrary")), name="dilated_prompt",
    )(q, k, v, tbl)


def _dil_sample_kernel(qbd_ref, kc_ref, kn_ref, vc_ref, vn_ref, tc_ref, tn_ref, o_ref, *, n_new):
    qbd = qbd_ref[...]
    scale = HEAD_DIM ** -0.5
    sc = _dot(qbd, kc_ref[...].astype(BF16)) * scale + tc_ref[...]
    sn = _dot_nt(qbd, kn_ref[...].astype(BF16)) * scale + tn_ref[...]
    m = jnp.maximum(jnp.max(sc, axis=1, keepdims=True), jnp.max(sn, axis=1, keepdims=True))
    pc = jnp.exp(sc - m)
    pn = jnp.exp(sn - m)
    l = jnp.sum(pc, axis=1, keepdims=True) + jnp.sum(pn, axis=1, keepdims=True)
    acc = _dot_nt(pc.astype(BF16), vc_ref[...].astype(BF16)) + _dot(pn.astype(BF16), vn_ref[...].astype(BF16))
    o_ref[...] = _block_diag_rows(acc / l, N_HEADS_C, n_new)


def _dilated_sample(q, k_new, v_new, cache_k, cache_v):
    db, ds, w = q.shape
    wb = cache_k.shape[1]
    rows = N_HEADS_C * ds
    qh = q.reshape(db, ds, N_HEADS_C, HEAD_DIM).transpose(0, 2, 1, 3)
    eye = jnp.eye(N_HEADS_C, dtype=q.dtype)
    qbd = (qh[:, :, :, None, :] * eye[None, :, None, :, None]).reshape(db, rows, w)
    kt = jnp.transpose(cache_k, (0, 2, 3, 1)).reshape(db, w, wb)
    vt = jnp.transpose(cache_v, (0, 2, 3, 1)).reshape(db, w, wb)
    pad = lambda a: jnp.pad(a, ((0, 0), (0, LANE - ds), (0, 0)))
    rel = (wb + np.arange(ds)[:, None]) - np.arange(wb + LANE)[None, :]
    bias = _mult_bias(rel)
    bias[:, wb + ds:] = NEG
    tbl = np.tile(bias, (N_HEADS_C, 1))
    per_b = lambda *shape: pl.BlockSpec((None,) + shape, lambda b: (b,) + (0,) * len(shape))
    const = lambda a: pl.BlockSpec(a.shape, lambda b: (0,) * a.ndim)
    tc, tn = jnp.asarray(tbl[:, :wb]), jnp.asarray(tbl[:, wb:])
    return pl.pallas_call(
        functools.partial(_dil_sample_kernel, n_new=ds), grid=(db,),
        in_specs=[per_b(rows, w), per_b(w, wb), per_b(LANE, w), per_b(w, wb), per_b(LANE, w), const(tc), const(tn)],
        out_specs=per_b(ds, w),
        out_shape=jax.ShapeDtypeStruct((db, ds, w), F32),
        compiler_params=_cparams(("arbitrary",)), name="dilated_sample",
    )(qbd, kt, pad(k_new), vt, pad(v_new), tc, tn)


def kernel(x_prompt, x_sample, c_prompt, c_sample, cache_a_k, cache_a_v, cache_a_idx_k, state_conv, cache_c_k, cache_c_v, page_table, norm_mix_g, norm_ffn_g, w_ada, b_ada, w_in_even, w_out_even, a_q_norm_g, a_k_norm_g, b_conv_w, b_conv_b, b_ln_g, b_ln_b, w_in_odd, w_out_odd, c_q_norm_g, c_k_norm_g, w_router, b_router, w_gate_e, w_up_e, w_down_e, w_gate_s, w_up_s, w_down_s):
    bp, sp, d = x_prompt.shape
    bs, ss, _ = x_sample.shape
    depth = w_ada.shape[0]
    past = page_table.shape[1] * cache_a_k.shape[2]
    win_buf = cache_c_k.shape[2]
    rp, rs = bp * sp, bs * ss

    tab_p = _rope_tables(jnp.arange(sp, dtype=I32))
    tab_s = _rope_tables(jnp.tile(past + jnp.arange(ss, dtype=I32), bs))
    row = lambda a: a.reshape(1, -1)
    gain2 = lambda g: jnp.tile(g, 2).reshape(1, LANE)

    xp = x_prompt.reshape(rp, d)
    xs = x_sample.reshape(rs, d)
    c_all = jnp.concatenate([c_prompt, c_sample], axis=0)
    c_all = jnp.pad(c_all, ((0, -c_all.shape[0] % 8), (0, 0)))

    ak_p, av_p, ai_p, ak_s, av_s, ai_s, cv_p, cv_s = [], [], [], [], [], [], [], []
    ck_p, cw_p, ck_s, cw_s = [], [], [], []
    for layer in range(depth):
        mod = _adaln(c_all, w_ada[layer], b_ada[layer])
        mod_p = [m.reshape(bp, 1, d) for m in jnp.split(mod[:bp], 6, axis=-1)]
        mod_s = [jnp.repeat(m, ss, axis=0) for m in jnp.split(mod[bp:bp + bs], 6, axis=-1)]
        gmix, gffn = row(norm_mix_g[layer]), row(norm_ffn_g[layer])
        if layer % 2 == 0:
            e = layer // 2
            w_in = _perm_w_even(w_in_even[e])
            w_out = w_out_even[e].astype(BF16)
            qg, kg = gain2(a_q_norm_g[e]), gain2(a_k_norm_g[e])
            outs = [(A_W, BF16), (A_W, F32), (A_W, BF16), (A_W, F32), (A_W, BF16), (2 * LANE, BF16),
                    (LANE, F32), (LANE, BF16), (LANE, F32), (CONV_CH, F32)]
            q, k, kb, v, vb, qi, ki2, ki2b, wi, u = _project(
                _proj_even_kernel, xp, gmix, mod_p[0], mod_p[1], w_in, tab_p, qg, kg, sp, outs)
            b3 = lambda a: a.reshape(bp, sp, a.shape[-1])
            oa = _dsa_prompt(b3(q), b3(qi), b3(wi), b3(ki2b), b3(kb), b3(vb)).reshape(rp, A_W)
            u3 = b3(u)
            ob = _conv_module(u3, jnp.zeros((bp, CONV_WIDTH - 1, CONV_CH), F32),
                              b_conv_w[e], b_conv_b[e], b_ln_g[e], b_ln_b[e]).reshape(rp, CONV_CH)
            lhs_p = [oa, ob]
            ak_p.append(k.reshape(bp, sp, N_HEADS_A, HEAD_DIM))
            av_p.append(v.reshape(bp, sp, N_HEADS_A, HEAD_DIM))
            ai_p.append(ki2[:, :IDX_DIM].reshape(bp, sp, IDX_DIM))
            xpad = jnp.concatenate([jnp.zeros((bp, CONV_WIDTH - 1, CONV_CH), F32), u3], axis=1)
            cv_p.append(xpad[:, -(CONV_WIDTH - 1):])
            q, k, kb, v, vb, qi, ki2, ki2b, wi, u = _project(
                _proj_even_kernel, xs, gmix, mod_s[0], mod_s[1], w_in, tab_s, qg, kg, rs, outs)
            s3 = lambda a: a.reshape(bs, ss, a.shape[-1])
            oa = _dsa_sample(s3(q), s3(k), s3(v), s3(qi), s3(ki2)[:, :, :IDX_DIM], s3(wi)[:, :, :N_IDX_HEADS],
                             cache_a_k[e], cache_a_v[e], cache_a_idx_k[e], page_table)
            u3 = s3(u)
            ob = _conv_module(u3, state_conv[e], b_conv_w[e], b_conv_b[e], b_ln_g[e], b_ln_b[e])
            lhs_s = [oa.reshape(rs, A_W).astype(BF16), ob.reshape(rs, CONV_CH)]
            ak_s.append(k.reshape(bs, ss, N_HEADS_A, HEAD_DIM))
            av_s.append(v.reshape(bs, ss, N_HEADS_A, HEAD_DIM))
            ai_s.append(ki2[:, :IDX_DIM].reshape(bs, ss, IDX_DIM))
            cv_s.append(jnp.concatenate([state_conv[e], u3], axis=1)[:, -(CONV_WIDTH - 1):])
            ws = [w_out[:A_W], w_out[A_W:]]
        else:
            o = layer // 2
            w_in = w_in_odd[o].astype(BF16)
            qg, kg = gain2(c_q_norm_g[o]), gain2(c_k_norm_g[o])
            outs = [(C_W, BF16), (C_W, F32), (C_W, BF16), (C_W, F32), (C_W, BF16)]
            q, k, kb, v, vb = _project(_proj_odd_kernel, xp, gmix, mod_p[0], mod_p[1], w_in, tab_p, qg, kg, sp, outs)
            b3 = lambda a: a.reshape(bp, sp, a.shape[-1])
            lhs_p = [_dilated_prompt(b3(q), b3(kb), b3(vb)).reshape(rp, C_W)]
            k4, v4 = k.reshape(bp, sp, N_HEADS_C, HEAD_DIM), v.reshape(bp, sp, N_HEADS_C, HEAD_DIM)
            padt = max(win_buf - sp, 0)
            tail = lambda a: jnp.pad(a, ((0, 0), (padt, 0), (0, 0), (0, 0)))[:, -win_buf:]
            ck_p.append(tail(k4))
            cw_p.append(tail(v4))
            q, k, kb, v, vb = _project(_proj_odd_kernel, xs, gmix, mod_s[0], mod_s[1], w_in, tab_s, qg, kg, rs, outs)
            s3 = lambda a: a.reshape(bs, ss, a.shape[-1])
            att = _dilated_sample(s3(q), s3(k), s3(v), cache_c_k[o], cache_c_v[o])
            lhs_s = [att.reshape(rs, C_W).astype(BF16)]
            k4, v4 = k.reshape(bs, ss, N_HEADS_C, HEAD_DIM), v.reshape(bs, ss, N_HEADS_C, HEAD_DIM)
            ck_s.append(jnp.concatenate([cache_c_k[o], k4], axis=1)[:, -win_buf:])
            cw_s.append(jnp.concatenate([cache_c_v[o], v4], axis=1)[:, -win_buf:])
            ws = [w_out_odd[o].astype(BF16)]

        wg, wu, wd = w_gate_e[layer].astype(BF16), w_up_e[layer].astype(BF16), w_down_e[layer].astype(BF16)
        wgs, wus, wds = w_gate_s[layer].astype(BF16), w_up_s[layer].astype(BF16), w_down_s[layer].astype(BF16)
        xn, h, idx, gate = _outproj_router(lhs_p, ws, xp, mod_p[2], gffn, mod_p[3], mod_p[4],
                                           w_router[layer], b_router[layer], sp)
        xp = _moe(h, xn, mod_p[5], idx[:, :TOP_K], gate[:, :TOP_K], wg, wu, wd, wgs, wus, wds, sp)
        xn, h, idx, gate = _outproj_router(lhs_s, ws, xs, mod_s[2], gffn, mod_s[3], mod_s[4],
                                           w_router[layer], b_router[layer], rs)
        xs = _moe(h, xn, mod_s[5], idx[:, :TOP_K], gate[:, :TOP_K], wg, wu, wd, wgs, wus, wds, rs)

    return (xp.reshape(bp, sp, d), xs.reshape(bs, ss, d),
            jnp.stack(ak_p), jnp.stack(av_p), jnp.stack(ai_p),
            jnp.stack(ak_s), jnp.stack(av_s), jnp.stack(ai_s),
            jnp.stack(cv_p), jnp.stack(cv_s),
            jnp.stack(ck_p), jnp.stack(cw_p), jnp.stack(ck_s), jnp.stack(cw_s))
```

```python
import functools

import numpy as np
import jax
import jax.numpy as jnp
from jax import lax
from jax.experimental import pallas as pl
from jax.experimental.pallas import tpu as pltpu

F32, BF16, I32 = jnp.float32, jnp.bfloat16, jnp.int32

HEAD_DIM = 64
N_HEADS_A = 8
CONV_CH = 512
CONV_WIDTH = 31
N_IDX_HEADS = 4
IDX_DIM = 64
TOPK_MAX = 256
N_HEADS_C = 16
DILATED_PATTERNS = ((128, 1), (512, 4), (2048, 16))
N_EXPERTS = 64
TOP_K = 6
ROUTED_SCALE = 2.5
ROPE_THETA = 10000.0
EPS = 1e-6

LANE = 128
ROW_GROUP = 8
DSA_Q_TILE = 256
DIL_Q_TILE = 512
MOE_TOKEN_TILE = 2048
MOE_ROW_CHUNK = 256
NEG = -1e30
KEY_NEG_INF = -2139095041
VMEM_LIMIT = 60 * 1024 * 1024

A_W = N_HEADS_A * HEAD_DIM
C_W = N_HEADS_C * HEAD_DIM


def _cparams(sem):
    return pltpu.CompilerParams(dimension_semantics=sem, vmem_limit_bytes=VMEM_LIMIT)


def _dot(a, b):
    return jnp.dot(a, b, preferred_element_type=F32)


def _dot_nt(a, b):
    return lax.dot_general(a, b, (((1,), (1,)), ((), ())), preferred_element_type=F32)


def _dot_tn(a, b):
    return lax.dot_general(a, b, (((0,), (0,)), ((), ())), preferred_element_type=F32)


def _silu(x):
    return x * jax.nn.sigmoid(x)


def _rms_mod(x, g, shift, scale):
    ms = jnp.mean(x * x, axis=-1, keepdims=True)
    y = x * lax.rsqrt(ms + EPS) * g
    return y * (1.0 + scale) + shift


def _to_key(x):
    b = lax.bitcast_convert_type(x, I32)
    k = jnp.where(b < 0, b ^ 0x7FFFFFFF, b)
    return jnp.where(k == -1, 0, k)


def _rope_tables(pos):
    half = HEAD_DIM // 2
    inv = ROPE_THETA ** (-jnp.arange(half, dtype=F32) / half)
    ang = pos.astype(F32)[:, None] * inv[None, :]
    c, s = jnp.cos(ang), jnp.sin(ang)
    z = jnp.zeros_like(s)
    cos = jnp.tile(c, (1, 4))
    sin_a = jnp.tile(jnp.concatenate([z, s], axis=1), (1, 2))
    sin_b = jnp.tile(jnp.concatenate([-s, z], axis=1), (1, 2))
    return cos, sin_a, sin_b


def _head_sum_matrix():
    r = np.arange(LANE)
    return jnp.asarray((r[:, None] // HEAD_DIM == r[None, :] // HEAD_DIM).astype(np.float32), BF16)


def _prefix_matrix(n):
    r = np.arange(n)
    return jnp.asarray((r[:, None] <= r[None, :]).astype(np.float32), BF16)


def _mult_bias(rel):
    mult = np.zeros(rel.shape, np.int64)
    for w, d in DILATED_PATTERNS:
        mult += ((rel >= 0) & (rel <= w) & (rel % d == 0))
    return np.where(mult > 0, np.log(np.maximum(mult, 1)), NEG).astype(np.float32)


def _ada_kernel(c_ref, w_ref, b_ref, o_ref):
    a = _silu(c_ref[...]).astype(BF16)
    o_ref[...] = _dot(a, w_ref[...].astype(BF16)) + b_ref[...]


def _adaln(c, w, b):
    m, d = c.shape
    n = w.shape[1]
    tn = 1536
    return pl.pallas_call(
        _ada_kernel, grid=(n // tn,),
        in_specs=[pl.BlockSpec((m, d), lambda j: (0, 0)),
                  pl.BlockSpec((d, tn), lambda j: (0, j)),
                  pl.BlockSpec((1, tn), lambda j: (0, j))],
        out_specs=pl.BlockSpec((m, tn), lambda j: (0, j)),
        out_shape=jax.ShapeDtypeStruct((m, n), F32),
        compiler_params=_cparams(("arbitrary",)), name="adaln")(c, w, b.reshape(1, n))


def _norm_rope(z, bsum, g, cos, sin_a, sin_b, norm):
    if norm:
        sq = z * z
        hi = sq.astype(BF16)
        lo = (sq - hi.astype(F32)).astype(BF16)
        ms = (_dot(hi, bsum) + _dot(lo, bsum)) * (1.0 / HEAD_DIM)
        z = z * lax.rsqrt(ms + EPS) * g
    return z * cos + pltpu.roll(z, 32, 1) * sin_a + pltpu.roll(z, 96, 1) * sin_b


def _proj_even_kernel(x_ref, gn_ref, sh_ref, sc_ref, w_ref, cos_ref, sa_ref, sb_ref, qg_ref, kg_ref, bsum_ref,
                      q_ref, k_ref, kb_ref, v_ref, vb_ref, qi_ref, ki_ref, kib_ref, wi_ref, u_ref, h_scr):
    h_scr[...] = _rms_mod(x_ref[...], gn_ref[...], sh_ref[...], sc_ref[...]).astype(BF16)
    hb = h_scr[...]
    cos, sa, sb, bsum = cos_ref[...], sa_ref[...], sb_ref[...], bsum_ref[...]

    def seg(lo, n):
        return _dot(hb, w_ref[:, lo:lo + n])

    z = seg(0, A_W)
    for c in range(A_W // LANE):
        sl = slice(c * LANE, (c + 1) * LANE)
        q_ref[:, sl] = _norm_rope(z[:, sl], bsum, qg_ref[...], cos, sa, sb, True).astype(BF16)
    z = seg(A_W, A_W)
    for c in range(A_W // LANE):
        sl = slice(c * LANE, (c + 1) * LANE)
        kk = _norm_rope(z[:, sl], bsum, kg_ref[...], cos, sa, sb, True)
        k_ref[:, sl] = kk
        kb_ref[:, sl] = kk.astype(BF16)
    z = seg(2 * A_W, A_W)
    v_ref[...] = z
    vb_ref[...] = z.astype(BF16)
    z = seg(3 * A_W, 2 * LANE)
    for c in range(2):
        sl = slice(c * LANE, (c + 1) * LANE)
        qi_ref[:, sl] = _norm_rope(z[:, sl], bsum, None, cos, sa, sb, False).astype(BF16)
    z = _norm_rope(seg(3 * A_W + 2 * LANE, LANE), bsum, None, cos, sa, sb, False)
    ki_ref[...] = z
    kib_ref[...] = z.astype(BF16)
    off = 3 * A_W + 3 * LANE
    ua = seg(off, CONV_CH)
    ub = seg(off + CONV_CH, CONV_CH)
    u_ref[...] = ua * jax.nn.sigmoid(ub)
    wi_ref[...] = seg(off + 2 * CONV_CH, LANE) * (N_IDX_HEADS ** -0.5 * IDX_DIM ** -0.5)


def _proj_odd_kernel(x_ref, gn_ref, sh_ref, sc_ref, w_ref, cos_ref, sa_ref, sb_ref, qg_ref, kg_ref, bsum_ref,
                     q_ref, k_ref, kb_ref, v_ref, vb_ref, h_scr):
    h_scr[...] = _rms_mod(x_ref[...], gn_ref[...], sh_ref[...], sc_ref[...]).astype(BF16)
    hb = h_scr[...]
    cos, sa, sb, bsum = cos_ref[...], sa_ref[...], sb_ref[...], bsum_ref[...]
    for half in range(2):
        z = _dot(hb, w_ref[:, half * A_W:(half + 1) * A_W])
        for c in range(A_W // LANE):
            sl = slice(c * LANE, (c + 1) * LANE)
            dl = slice(half * A_W + c * LANE, half * A_W + (c + 1) * LANE)
            q_ref[:, dl] = _norm_rope(z[:, sl], bsum, qg_ref[...], cos, sa, sb, True).astype(BF16)
    for half in range(2):
        z = _dot(hb, w_ref[:, C_W + half * A_W:C_W + (half + 1) * A_W])
        for c in range(A_W // LANE):
            sl = slice(c * LANE, (c + 1) * LANE)
            dl = slice(half * A_W + c * LANE, half * A_W + (c + 1) * LANE)
            kk = _norm_rope(z[:, sl], bsum, kg_ref[...], cos, sa, sb, True)
            k_ref[:, dl] = kk
            kb_ref[:, dl] = kk.astype(BF16)
    for half in range(2):
        z = _dot(hb, w_ref[:, 2 * C_W + half * A_W:2 * C_W + (half + 1) * A_W])
        v_ref[:, half * A_W:(half + 1) * A_W] = z
        vb_ref[:, half * A_W:(half + 1) * A_W] = z.astype(BF16)


def _mod_spec(mod, tr, tiles_per_group):
    d = mod.shape[-1]
    if mod.ndim == 3:
        return pl.BlockSpec((None, 1, d), lambda i: (i // tiles_per_group, 0, 0))
    return pl.BlockSpec((tr, d), lambda i: (i, 0))


def _row_tile(rows, want):
    return want if rows % want == 0 else rows


def _project(kern, x, gn, shift, scale, w, tables, qg, kg, group_rows, outs):
    r, d = x.shape
    tr = _row_tile(min(r, group_rows), 512)
    tpg = max(group_rows // tr, 1)
    tpt = tables[0].shape[0] // tr
    const = lambda a: pl.BlockSpec(a.shape, lambda i: (0,) * a.ndim)
    tab = pl.BlockSpec((tr, LANE), lambda i: (i % tpt, 0))
    bsum = _head_sum_matrix()
    return pl.pallas_call(
        kern, grid=(r // tr,),
        in_specs=[pl.BlockSpec((tr, d), lambda i: (i, 0)), const(gn), _mod_spec(shift, tr, tpg),
                  _mod_spec(scale, tr, tpg), const(w), tab, tab, tab, const(qg), const(kg), const(bsum)],
        out_specs=[pl.BlockSpec((tr, wd), lambda i: (i, 0)) for wd, _ in outs],
        out_shape=[jax.ShapeDtypeStruct((r, wd), dt) for wd, dt in outs],
        scratch_shapes=[pltpu.VMEM((tr, d), BF16)],
        compiler_params=_cparams(("arbitrary",)), name=kern.__name__.strip("_"),
    )(x, gn, shift, scale, w, *tables, qg, kg, bsum)


def _perm_w_even(w):
    d = w.shape[0]
    cuts = np.cumsum([A_W, A_W, A_W, N_IDX_HEADS * IDX_DIM, IDX_DIM, N_IDX_HEADS, CONV_CH])
    q, k, v, qi, ki, wi, ua, ub = jnp.split(w, [int(c) for c in cuts], axis=1)
    pad = jnp.zeros((d, LANE - N_IDX_HEADS), w.dtype)
    return jnp.concatenate([q, k, v, qi, ki, ki, ua, ub, wi, pad], axis=1).astype(BF16)


def _dsa_prompt_kernel(q_ref, qi_ref, wi_ref, ki_ref, k_ref, v_ref, u_ref, o_ref, key_ref, bias_ref,
                       qs_ref, m_ref, l_ref, acc_ref, *, tq, kc, topk):
    i = pl.program_id(1)
    q0 = i * tq
    cd = q0 // kc
    lane = lax.broadcasted_iota(I32, (tq, LANE), 1)
    lo_half = lane < HEAD_DIM
    zero = jnp.zeros((tq, LANE), BF16)

    qi = qi_ref[...]
    qm = [jnp.where(lo_half if h % 2 == 0 else ~lo_half, qi[:, (h // 2) * LANE:(h // 2 + 1) * LANE], zero)
          for h in range(N_IDX_HEADS)]
    wi = wi_ref[...]
    wcol = [wi[:, h:h + 1] for h in range(N_IDX_HEADS)]

    def score_chunk(c):
        kk = ki_ref[pl.ds(pl.multiple_of(c * kc, kc), kc), :]
        acc = None
        for h in range(N_IDX_HEADS):
            t = wcol[h] * jnp.maximum(_dot_nt(qm[h], kk), 0.0)
            acc = t if acc is None else acc + t
        return acc

    def fill(c, _):
        key_ref[c] = _to_key(score_chunk(c))
        return 0

    lax.fori_loop(0, cd, fill, 0)
    kpos = cd * kc + lax.broadcasted_iota(I32, (tq, kc), 1)
    qpos = q0 + lax.broadcasted_iota(I32, (tq, kc), 0)
    key_ref[cd] = _to_key(jnp.where(kpos <= qpos, score_chunk(cd), -jnp.inf))

    def count(cand, strict):
        def body(c, cnt):
            kk = key_ref[c]
            one = jnp.where((kk > cand) if strict else (kk >= cand), 1.0, 0.0)
            part = one[:, 0:LANE]
            for j in range(1, kc // LANE):
                part = part + one[:, j * LANE:(j + 1) * LANE]
            return cnt + part
        cnt = lax.fori_loop(0, cd + 1, body, jnp.zeros((tq, LANE), F32))
        return jnp.sum(cnt, axis=1, keepdims=True)

    kf = float(topk)
    thr = jnp.where(count(jnp.zeros((tq, 1), I32), False) >= kf, 0, -2 ** 31).astype(I32)

    def bit(b, thr):
        cand = thr + jnp.left_shift(jnp.int32(1), 30 - b)
        return jnp.where(count(cand, False) >= kf, cand, thr)

    thr = lax.fori_loop(0, 31, bit, thr)
    need = kf - count(thr, True)
    upper = u_ref[...]

    def select(c, run):
        kk = key_ref[c]
        tie = kk == thr
        pre = _dot(jnp.where(tie, 1.0, 0.0).astype(BF16), upper) + run
        sel = ((kk > thr) | (tie & (pre <= need))) & (kk > KEY_NEG_INF)
        bias_ref[c] = jnp.where(sel, 0.0, NEG)
        return pre[:, kc - 1:kc]

    lax.fori_loop(0, cd + 1, select, jnp.zeros((tq, 1), F32))

    n_pairs = A_W // LANE
    for hp in range(n_pairs):
        qs_ref[hp] = _stack_pair(q_ref[:, hp * LANE:(hp + 1) * LANE], lo_half)
    _flash_init(m_ref, l_ref, acc_ref)

    def attend(c, _):
        rows = pl.ds(pl.multiple_of(c * kc, kc), kc)
        b = bias_ref[c]
        bias2 = jnp.concatenate([b, b], axis=0)
        for hp in range(n_pairs):
            cols = slice(hp * LANE, (hp + 1) * LANE)
            _flash_step(qs_ref[hp], k_ref[rows, cols], v_ref[rows, cols], bias2, m_ref, l_ref, acc_ref, hp)
        return 0

    lax.fori_loop(0, cd + 1, attend, 0)
    for hp in range(n_pairs):
        o_ref[:, hp * LANE:(hp + 1) * LANE] = _flash_finish(l_ref, acc_ref, hp, lo_half, tq).astype(o_ref.dtype)


def _stack_pair(qp, lo_half):
    qsc = (qp.astype(F32) * (HEAD_DIM ** -0.5)).astype(BF16)
    zero = jnp.zeros_like(qsc)
    return jnp.concatenate([jnp.where(lo_half, qsc, zero), jnp.where(lo_half, zero, qsc)], axis=0)


def _flash_init(m_ref, l_ref, acc_ref):
    m_ref[...] = jnp.full(m_ref.shape, NEG, F32)
    l_ref[...] = jnp.zeros(l_ref.shape, F32)
    acc_ref[...] = jnp.zeros(acc_ref.shape, F32)


def _flash_step(qs, kblk, vblk, bias2, m_ref, l_ref, acc_ref, idx):
    s = _dot_nt(qs, kblk) + bias2
    m_old = m_ref[idx]
    mn = jnp.maximum(m_old, jnp.max(s, axis=1, keepdims=True))
    p = jnp.exp(s - jnp.tile(mn, (1, s.shape[1] // LANE)))
    a = jnp.exp(m_old - mn)
    l_ref[idx] = a * l_ref[idx] + jnp.sum(p, axis=1, keepdims=True)
    acc_ref[idx] = a * acc_ref[idx] + _dot(p.astype(BF16), vblk)
    m_ref[idx] = mn


def _flash_finish(l_ref, acc_ref, idx, lo_half, t):
    o = acc_ref[idx] / l_ref[idx]
    return jnp.where(lo_half, o[:t], o[t:])


def _dsa_prompt(q, qi, wi, ki2, k, v):
    b, s, _ = q.shape
    tq = min(DSA_Q_TILE, s)
    kc = min(512, s)
    topk = min(TOPK_MAX, s // 4)
    nc = s // kc
    upper = _prefix_matrix(kc)
    qspec = lambda w: pl.BlockSpec((None, tq, w), lambda bi, i: (bi, i, 0))
    kspec = lambda w: pl.BlockSpec((None, s, w), lambda bi, i: (bi, 0, 0), pipeline_mode=pl.Buffered(1))
    return pl.pallas_call(
        functools.partial(_dsa_prompt_kernel, tq=tq, kc=kc, topk=topk),
        grid=(b, s // tq),
        in_specs=[qspec(A_W), qspec(2 * LANE), qspec(LANE), kspec(LANE), kspec(A_W), kspec(A_W),
                  pl.BlockSpec((kc, kc), lambda bi, i: (0, 0))],
        out_specs=qspec(A_W),
        out_shape=jax.ShapeDtypeStruct((b, s, A_W), BF16),
        scratch_shapes=[pltpu.VMEM((nc, tq, kc), I32), pltpu.VMEM((nc, tq, kc), F32),
                        pltpu.VMEM((A_W // LANE, 2 * tq, LANE), BF16)]
        + [pltpu.VMEM((A_W // LANE, 2 * tq, LANE), F32)] * 3,
        compiler_params=_cparams(("arbitrary", "arbitrary")), name="dsa_prompt",
    )(q, qi, wi, ki2, k, v, upper)


def _dsa_sample_select_kernel(pt_ref, qi_ref, wi_ref, *rest, n_new, npg, pg, topk):
    del pt_ref
    kip_refs = rest[:pg]
    kin_ref, u_ref, o_ref, key_ref = rest[pg:]
    p = pl.program_id(1)
    qi = qi_ref[...]
    wi = wi_ref[...]
    n_chunks = key_ref.shape[0]

    def score(dots):
        r = jnp.maximum(dots, 0.0) * wi
        acc = r[0:n_new]
        for h in range(1, N_IDX_HEADS):
            acc = acc + r[h * n_new:(h + 1) * n_new]
        return acc

    for g in range(pg):
        key_ref[p * pg + g] = _to_key(score(_dot(qi, kip_refs[g][...].astype(BF16))))

    @pl.when(p == 0)
    def _():
        for c in range(npg + 1, n_chunks):
            key_ref[c] = jnp.full((n_new, LANE), -2 ** 31, I32)

    @pl.when(p == npg // pg - 1)
    def _():
        col = lax.broadcasted_iota(I32, (n_new, LANE), 1)
        row = lax.broadcasted_iota(I32, (n_new, LANE), 0)
        new = score(_dot_nt(qi, kin_ref[...].astype(BF16)))
        key_ref[npg] = _to_key(jnp.where(col <= row, new, -jnp.inf))

        def count(cand, strict):
            def body(c, cnt):
                kk = key_ref[pl.ds(pl.multiple_of(c * ROW_GROUP, ROW_GROUP), ROW_GROUP)]
                hit = jnp.where((kk > cand) if strict else (kk >= cand), 1.0, 0.0)
                return cnt + jnp.sum(hit, axis=0)
            cnt = lax.fori_loop(0, n_chunks // ROW_GROUP, body, jnp.zeros((n_new, LANE), F32))
            return jnp.sum(cnt, axis=1, keepdims=True)

        kf = float(topk)
        thr = jnp.where(count(jnp.zeros((n_new, 1), I32), False) >= kf, 0, -2 ** 31).astype(I32)

        def bit(b, thr):
            cand = thr + jnp.left_shift(jnp.int32(1), 30 - b)
            return jnp.where(count(cand, False) >= kf, cand, thr)

        thr = lax.fori_loop(0, 31, bit, thr)
        need = kf - count(thr, True)
        upper = u_ref[...]

        def select(g, run):
            c0 = pl.multiple_of(g * ROW_GROUP, ROW_GROUP)
            kk = key_ref[pl.ds(c0, ROW_GROUP)]
            tie = kk == thr
            ties = jnp.where(tie, 1.0, 0.0).reshape(ROW_GROUP * n_new, LANE).astype(BF16)
            pre = _dot(ties, upper).reshape(ROW_GROUP, n_new, LANE)
            for c in range(ROW_GROUP):
                pc = pre[c] + run
                sel = ((kk[c] > thr) | (tie[c] & (pc <= need))) & (kk[c] > KEY_NEG_INF)
                o_ref[c0 + c] = jnp.where(sel, 1.0, 0.0)
                run = pc[:, LANE - 1:LANE]
            return run

        lax.fori_loop(0, n_chunks // ROW_GROUP, select, jnp.zeros((n_new, 1), F32))


def _block_diag_rows(o_full, n_heads, n_new):
    head_of_lane = lax.broadcasted_iota(I32, (n_new, o_full.shape[1]), 1) // HEAD_DIM
    out = jnp.zeros((n_new, o_full.shape[1]), F32)
    for h in range(n_heads):
        out = jnp.where(head_of_lane == h, o_full[h * n_new:(h + 1) * n_new], out)
    return out


def _dsa_sample_attend_kernel(pt_ref, qbd_ref, *rest, npg, pg, n_new):
    del pt_ref
    kp_refs, vp_refs = rest[:pg], rest[pg:2 * pg]
    kn_ref, vn_ref, sel_ref, o_ref, m_ref, l_ref, acc_ref = rest[2 * pg:]
    p = pl.program_id(1)
    qbd = qbd_ref[...]

    @pl.when(p == 0)
    def _():
        _flash_init(m_ref, l_ref, acc_ref)

    def bias(sel):
        return (jnp.concatenate([sel] * N_HEADS_A, axis=0) - 1.0) * (-NEG)

    def update(scores, pv):
        m = m_ref[...]
        mn = m
        for s in scores:
            mn = jnp.maximum(mn, jnp.max(s, axis=1, keepdims=True))
        a = jnp.exp(m - mn)
        l = a * l_ref[...]
        acc = jnp.tile(a, (1, A_W // LANE)) * acc_ref[...]
        for s, f in zip(scores, pv):
            pr = jnp.exp(s - mn)
            l = l + jnp.sum(pr, axis=1, keepdims=True)
            acc = acc + f(pr.astype(BF16))
        m_ref[...] = mn
        l_ref[...] = l
        acc_ref[...] = acc

    scale = HEAD_DIM ** -0.5
    update([_dot(qbd, kp_refs[g][...].astype(BF16)) * scale + bias(sel_ref[p * pg + g]) for g in range(pg)],
           [lambda pr, g=g: _dot_nt(pr, vp_refs[g][...].astype(BF16)) for g in range(pg)])

    @pl.when(p == npg // pg - 1)
    def _():
        update([_dot_nt(qbd, kn_ref[...].astype(BF16)) * scale + bias(sel_ref[npg])],
               [lambda pr: _dot(pr, vn_ref[...].astype(BF16))])
        o_full = acc_ref[...] / jnp.tile(l_ref[...], (1, A_W // LANE))
        o_ref[...] = _block_diag_rows(o_full, N_HEADS_A, n_new)


def _dsa_sample(q, k_new, v_new, qi, ki_new, wi, pool_k, pool_v, pool_ki, page_table):
    db, ds, _ = q.shape
    n_phys, page = pool_k.shape[0], pool_k.shape[1]
    npg = page_table.shape[1]
    topk = min(TOPK_MAX, (npg * page + ds) // 4)
    pt = page_table.reshape(-1).astype(I32)
    hq = N_IDX_HEADS * ds
    qi_r = qi.reshape(db, ds, N_IDX_HEADS, IDX_DIM).transpose(0, 2, 1, 3).reshape(db, hq, IDX_DIM)
    wi_r = jnp.broadcast_to(wi.transpose(0, 2, 1).reshape(db, hq, 1), (db, hq, LANE))
    pad = lambda a: jnp.pad(a, ((0, 0), (0, page - ds), (0, 0)))
    per_b = lambda *shape: pl.BlockSpec((None,) + shape, lambda b, p, pt: (b,) + (0,) * len(shape))
    pg = next(g for g in (8, 4, 2, 1) if npg % g == 0)
    paged = lambda w: [pl.BlockSpec((None, w, page), lambda b, p, pt, g=g: (pt[b * npg + p * pg + g], 0, 0))
                       for g in range(pg)]
    pool_kt = jnp.transpose(pool_k, (0, 2, 3, 1)).reshape(n_phys, A_W, page)
    pool_vt = jnp.transpose(pool_v, (0, 2, 3, 1)).reshape(n_phys, A_W, page)
    pool_kit = jnp.transpose(pool_ki, (0, 2, 1))
    n_chunks = -(-(npg + 1) // ROW_GROUP) * ROW_GROUP
    sel = pl.pallas_call(
        functools.partial(_dsa_sample_select_kernel, n_new=ds, npg=npg, pg=pg, topk=topk),
        grid_spec=pltpu.PrefetchScalarGridSpec(
            num_scalar_prefetch=1, grid=(db, npg // pg),
            in_specs=[per_b(hq, IDX_DIM), per_b(hq, LANE)] + paged(IDX_DIM)
            + [per_b(page, IDX_DIM), pl.BlockSpec((page, page), lambda b, p, pt: (0, 0))],
            out_specs=per_b(n_chunks, ds, page),
            scratch_shapes=[pltpu.VMEM((n_chunks, ds, page), I32)]),
        out_shape=jax.ShapeDtypeStruct((db, n_chunks, ds, page), F32),
        compiler_params=_cparams(("arbitrary", "arbitrary")), name="dsa_sample_select",
    )(pt, qi_r, wi_r, *([pool_kit] * pg), pad(ki_new), _prefix_matrix(page))

    rows = N_HEADS_A * ds
    qh = q.reshape(db, ds, N_HEADS_A, HEAD_DIM).transpose(0, 2, 1, 3)
    eye = jnp.eye(N_HEADS_A, dtype=q.dtype)
    qbd = (qh[:, :, :, None, :] * eye[None, :, None, :, None]).reshape(db, rows, A_W)
    state = pltpu.VMEM((rows, LANE), F32)
    return pl.pallas_call(
        functools.partial(_dsa_sample_attend_kernel, npg=npg, pg=pg, n_new=ds),
        grid_spec=pltpu.PrefetchScalarGridSpec(
            num_scalar_prefetch=1, grid=(db, npg // pg),
            in_specs=[per_b(rows, A_W)] + paged(A_W) + paged(A_W)
            + [per_b(page, A_W), per_b(page, A_W), per_b(n_chunks, ds, page)],
            out_specs=per_b(ds, A_W),
            scratch_shapes=[state, state, pltpu.VMEM((rows, A_W), F32)]),
        out_shape=jax.ShapeDtypeStruct((db, ds, A_W), F32),
        compiler_params=_cparams(("arbitrary", "arbitrary")), name="dsa_sample_attend",
    )(pt, qbd, *([pool_kt] * pg), *([pool_vt] * pg), pad(k_new), pad(v_new), sel)


def _conv_kernel(u_ref, halo_ref, buf_ref, w_ref, b_ref, g_ref, bb_ref, o_ref, xp_ref, *, tt):
    i = pl.program_id(1)

    @pl.when(i == 0)
    def _():
        xp_ref[0:32, :] = buf_ref[...]

    @pl.when(i > 0)
    def _():
        xp_ref[0:32, :] = halo_ref[...]

    xp_ref[32:32 + tt, :] = u_ref[...]
    w = w_ref[...]
    acc = jnp.zeros((tt, CONV_CH), F32) + b_ref[...]
    for j in range(CONV_WIDTH):
        acc = acc + xp_ref[2 + j:2 + j + tt, :] * w[j:j + 1, :]
    mu = jnp.mean(acc, axis=-1, keepdims=True)
    var = jnp.mean(jnp.square(acc - mu), axis=-1, keepdims=True)
    yn = (acc - mu) * lax.rsqrt(var + EPS) * g_ref[...] + bb_ref[...]
    o_ref[...] = _silu(yn).astype(o_ref.dtype)


def _conv_module(u, buf30, conv_w, conv_b, ln_g, ln_b):
    b, t, c = u.shape
    tt = _row_tile(t, 512)
    buf = jnp.pad(buf30, ((0, 0), (2, 0), (0, 0)))
    halo_src = u if t >= 32 else buf
    hb = tt // 32
    w = jnp.pad(conv_w, ((0, 32 - CONV_WIDTH), (0, 0)))
    const = lambda a: pl.BlockSpec(a.shape, lambda bi, i: (0,) * a.ndim)
    row = lambda a: a.reshape(1, c)
    return pl.pallas_call(
        functools.partial(_conv_kernel, tt=tt), grid=(b, t // tt),
        in_specs=[pl.BlockSpec((None, tt, c), lambda bi, i: (bi, i, 0)),
                  pl.BlockSpec((None, 32, c), lambda bi, i: (bi, jnp.maximum(i * hb - 1, 0), 0)),
                  pl.BlockSpec((None, 32, c), lambda bi, i: (bi, 0, 0)),
                  const(w), const(row(conv_b)), const(row(ln_g)), const(row(ln_b))],
        out_specs=pl.BlockSpec((None, tt, c), lambda bi, i: (bi, i, 0)),
        out_shape=jax.ShapeDtypeStruct((b, t, c), BF16),
        scratch_shapes=[pltpu.VMEM((32 + tt, c), F32)],
        compiler_params=_cparams(("arbitrary", "arbitrary")), name="conv_module",
    )(u, halo_src, buf, w, row(conv_b), row(ln_g), row(ln_b))


def _outproj_kernel(*refs, n_lhs):
    lhs, ws = refs[:n_lhs], refs[n_lhs:2 * n_lhs]
    x_ref, g2_ref, gn_ref, sh_ref, sc_ref, wr_ref, br_ref, xn_ref, h_ref, idx_ref, gate_ref = refs[2 * n_lhs:]
    mix = _dot(lhs[0][...], ws[0][...])
    for a, w in zip(lhs[1:], ws[1:]):
        mix = mix + _dot(a[...], w[...])
    xn = x_ref[...] + g2_ref[...] * mix
    xn_ref[...] = xn
    h = _rms_mod(xn, gn_ref[...], sh_ref[...], sc_ref[...])
    h_ref[...] = h
    s = jax.nn.sigmoid(_dot(h.astype(BF16), wr_ref[...]))
    tr = s.shape[0]
    lane = lax.broadcasted_iota(I32, (tr, LANE), 1)
    lanef = lane.astype(F32)
    vals = jnp.where(lane < N_EXPERTS, s + br_ref[...], -jnp.inf)
    idxm = jnp.zeros((tr, LANE), F32)
    gm = jnp.zeros((tr, LANE), F32)
    for j in range(TOP_K):
        m = jnp.max(vals, axis=1, keepdims=True)
        ix = jnp.min(jnp.where(vals == m, lanef, float(LANE)), axis=1, keepdims=True)
        hit = lanef == ix
        sj = jnp.sum(jnp.where(hit, s, 0.0), axis=1, keepdims=True)
        idxm = jnp.where(lane == j, ix, idxm)
        gm = jnp.where(lane == j, sj, gm)
        vals = jnp.where(hit, -jnp.inf, vals)
    idx_ref[...] = idxm.astype(I32)
    gate_ref[...] = gm / jnp.sum(gm, axis=1, keepdims=True) * ROUTED_SCALE


def _outproj_router(lhs, ws, x, gate2, gn, shift, scale, w_router, b_router, group_rows):
    r, d = x.shape
    tr = _row_tile(min(r, group_rows), 512)
    tpg = max(group_rows // tr, 1)
    const = lambda a: pl.BlockSpec(a.shape, lambda i: (0,) * a.ndim)
    rowblk = lambda w: pl.BlockSpec((tr, w), lambda i: (i, 0))
    wr = jnp.pad(w_router, ((0, 0), (0, LANE - N_EXPERTS))).astype(BF16)
    br = jnp.pad(b_router, (0, LANE - N_EXPERTS)).reshape(1, LANE)
    return pl.pallas_call(
        functools.partial(_outproj_kernel, n_lhs=len(lhs)), grid=(r // tr,),
        in_specs=[rowblk(a.shape[1]) for a in lhs] + [const(w) for w in ws]
        + [rowblk(d), _mod_spec(gate2, tr, tpg), const(gn), _mod_spec(shift, tr, tpg), _mod_spec(scale, tr, tpg),
           const(wr), const(br)],
        out_specs=[rowblk(d), rowblk(d), rowblk(LANE), rowblk(LANE)],
        out_shape=[jax.ShapeDtypeStruct((r, d), F32), jax.ShapeDtypeStruct((r, d), F32),
                   jax.ShapeDtypeStruct((r, LANE), I32), jax.ShapeDtypeStruct((r, LANE), F32)],
        compiler_params=_cparams(("arbitrary",)), name="outproj_router",
    )(*lhs, *ws, x, gate2, gn, shift, scale, wr, br)


def _moe_kernel(cnt_ref, off_ref, tok_ref, gate_ref, h_ref, x_ref, g5_ref, wg_ref, wu_ref, wd_ref,
                wgs_ref, wus_ref, wds_ref, o_ref, acc_ref, xg_ref, y_ref, *, t_rows, ch):
    t = pl.program_id(0)
    e = pl.program_id(1)
    ne = pl.num_programs(1)

    def ffn(xb, wg, wu, wd):
        a = _silu(_dot(xb, wg)) * _dot(xb, wu)
        return _dot(a.astype(BF16), wd)

    @pl.when(e == 0)
    def _():
        xg_ref[...] = jnp.zeros(xg_ref.shape, F32)
        acc_ref[t_rows:t_rows + ROW_GROUP, :] = jnp.zeros((ROW_GROUP, acc_ref.shape[1]), F32)

        def shared(c, _):
            rows = pl.ds(pl.multiple_of(c * ch, ch), ch)
            acc_ref[rows, :] = ffn(h_ref[rows, :].astype(BF16), wgs_ref[...], wus_ref[...], wds_ref[...])
            return 0

        lax.fori_loop(0, t_rows // ch, shared, 0)

    n = cnt_ref[t * ne + e]
    base = off_ref[t * ne + e]

    def chunk(c, _):
        r0 = base + c * ch
        rows = jnp.minimum(ch, n - c * ch)
        groups = lax.shift_right_logical(rows + (ROW_GROUP - 1), ROW_GROUP.bit_length() - 1)

        def gather(g, _):
            r = r0 + g * ROW_GROUP
            picked = [h_ref[pl.ds(tok_ref[r + j], 1), :] for j in range(ROW_GROUP)]
            xg_ref[pl.ds(pl.multiple_of(g * ROW_GROUP, ROW_GROUP), ROW_GROUP), :] = jnp.concatenate(picked, axis=0)
            return 0

        lax.fori_loop(0, groups, gather, 0)
        y_ref[...] = ffn(xg_ref[...].astype(BF16), wg_ref[...], wu_ref[...], wd_ref[...])

        def scatter(g, _):
            y8 = y_ref[pl.ds(pl.multiple_of(g * ROW_GROUP, ROW_GROUP), ROW_GROUP), :]
            dst, gate = [], []
            for j in range(ROW_GROUP):
                r = g * ROW_GROUP + j
                ok = r < rows
                dst.append(jnp.where(ok, tok_ref[r0 + r], t_rows + j))
                gate.append(jnp.where(ok, gate_ref[r0 + r], 0.0))
            new = [acc_ref[pl.ds(dst[j], 1), :] + gate[j] * y8[j:j + 1, :] for j in range(ROW_GROUP)]
            for j in range(ROW_GROUP):
                acc_ref[pl.ds(dst[j], 1), :] = new[j]
            return 0

        lax.fori_loop(0, groups, scatter, 0)
        return 0

    lax.fori_loop(0, (n + ch - 1) // ch, chunk, 0)

    @pl.when(e == ne - 1)
    def _():
        o_ref[...] = x_ref[...] + g5_ref[...] * acc_ref[0:t_rows, :]


def _moe(h, x, gate5, idx6, g6, wg, wu, wd, wgs, wus, wds, group_rows):
    r, d = h.shape
    t_rows = _row_tile(min(r, group_rows), MOE_TOKEN_TILE)
    ch = min(MOE_ROW_CHUNK, t_rows)
    nt = r // t_rows
    per = t_rows * TOP_K
    tpg = max(group_rows // t_rows, 1)
    e = idx6.reshape(nt, per)
    order = jnp.argsort(e, axis=-1, stable=True)
    per_pad = -(-(per + ROW_GROUP) // 1024) * 1024
    padl = lambda a: jnp.pad(a, ((0, 0), (0, per_pad - per))).reshape(-1)
    tok = padl((order // TOP_K).astype(I32))
    gs = padl(jnp.take_along_axis(g6.reshape(nt, per), order, axis=-1))
    cnt = jnp.sum((e[:, :, None] == jnp.arange(N_EXPERTS, dtype=I32)).astype(I32), axis=1)
    off = (jnp.cumsum(cnt, axis=-1) - cnt).astype(I32).reshape(-1)
    cnt = cnt.reshape(-1)
    if gate5.ndim == 3:
        g5spec = pl.BlockSpec((None, 1, d), lambda ti, ei, *_: (ti // tpg, 0, 0))
    else:
        g5spec = pl.BlockSpec((t_rows, d), lambda ti, ei, *_: (ti, 0))
    tile = pl.BlockSpec((t_rows, d), lambda ti, ei, *_: (ti, 0))
    tile_in = pl.BlockSpec((t_rows, d), lambda ti, ei, *_: (ti, 0), pipeline_mode=pl.Buffered(1))
    smem =pl.BlockSpec((per_pad,), lambda ti, ei, *_: (ti,), memory_space=pltpu.SMEM)
    expert = lambda a: pl.BlockSpec((None,) + a.shape[1:], lambda ti, ei, *_: (ei, 0, 0))
    const = lambda a: pl.BlockSpec(a.shape, lambda ti, ei, *_: (0,) * a.ndim)
    return pl.pallas_call(
        functools.partial(_moe_kernel, t_rows=t_rows, ch=ch),
        grid_spec=pltpu.PrefetchScalarGridSpec(
            num_scalar_prefetch=2, grid=(nt, N_EXPERTS),
            in_specs=[smem, smem, tile_in, tile_in, g5spec, expert(wg), expert(wu), expert(wd),
                      const(wgs), const(wus), const(wds)],
            out_specs=tile,
            scratch_shapes=[pltpu.VMEM((t_rows + ROW_GROUP, d), F32), pltpu.VMEM((ch, d), F32),
                            pltpu.VMEM((ch, d), F32)]),
        out_shape=jax.ShapeDtypeStruct((r, d), F32),
        compiler_params=_cparams(("arbitrary", "arbitrary")), name="moe",
    )(cnt, off, tok, gs, h, x, gate5, wg, wu, wd, wgs, wus, wds)


def _dil_prompt_kernel(q_ref, k_ref, v_ref, tbl_ref, o_ref, qs_ref, m_ref, l_ref, acc_ref, *, tq, nd, n_pairs):
    i = pl.program_id(2)
    lo_half = lax.broadcasted_iota(I32, (tq, LANE), 1) < HEAD_DIM
    for hp in range(n_pairs):
        qs_ref[hp] = _stack_pair(q_ref[:, hp * LANE:(hp + 1) * LANE], lo_half)
    _flash_init(m_ref, l_ref, acc_ref)

    def attend(j, _):
        rows = pl.ds(pl.multiple_of(j * tq, tq), tq)
        b = tbl_ref[i - j]
        bias2 = jnp.concatenate([b, b], axis=0)
        for hp in range(n_pairs):
            cols = slice(hp * LANE, (hp + 1) * LANE)
            _flash_step(qs_ref[hp], k_ref[rows, cols], v_ref[rows, cols], bias2, m_ref, l_ref, acc_ref, hp)
        return 0

    lax.fori_loop(jnp.maximum(i - (nd - 1), 0), i + 1, attend, 0)
    for hp in range(n_pairs):
        o_ref[:, hp * LANE:(hp + 1) * LANE] = _flash_finish(l_ref, acc_ref, hp, lo_half, tq).astype(o_ref.dtype)


def _dilated_prompt(q, k, v):
    b, s, w = q.shape
    tq = min(DIL_Q_TILE, s)
    nd = max(wd for wd, _ in DILATED_PATTERNS) // tq + 1
    dd = np.arange(nd)[:, None, None]
    rel = dd * tq + np.arange(tq)[None, :, None] - np.arange(tq)[None, None, :]
    tbl = jnp.asarray(_mult_bias(rel))
    n_pairs = 2
    gw = n_pairs * LANE
    state = pltpu.VMEM((n_pairs, 2 * tq, LANE), F32)
    return pl.pallas_call(
        functools.partial(_dil_prompt_kernel, tq=tq, nd=nd, n_pairs=n_pairs), grid=(b, w // gw, s // tq),
        in_specs=[pl.BlockSpec((None, tq, gw), lambda bi, hp, i: (bi, i, hp)),
                  pl.BlockSpec((None, s, gw), lambda bi, hp, i: (bi, 0, hp)),
                  pl.BlockSpec((None, s, gw), lambda bi, hp, i: (bi, 0, hp)),
                  pl.BlockSpec((nd, tq, tq), lambda bi, hp, i: (0, 0, 0))],
        out_specs=pl.BlockSpec((None, tq, gw), lambda bi, hp, i: (bi, i, hp)),
        out_shape=jax.ShapeDtypeStruct((b, s, w), BF16),
        scratch_shapes=[pltpu.VMEM((n_pairs, 2 * tq, LANE), BF16), state, state, state],
        compiler_params=_cparams(("arbitrary", "arbitrary", "arbitrary")), name="dilated_prompt",
    )(q, k, v, tbl)


def _dil_sample_kernel(qbd_ref, kc_ref, kn_ref, vc_ref, vn_ref, tc_ref, tn_ref, o_ref, *, n_new):
    qbd = qbd_ref[...]
    scale = HEAD_DIM ** -0.5
    sc = _dot(qbd, kc_ref[...].astype(BF16)) * scale + tc_ref[...]
    sn = _dot_nt(qbd, kn_ref[...].astype(BF16)) * scale + tn_ref[...]
    m = jnp.maximum(jnp.max(sc, axis=1, keepdims=True), jnp.max(sn, axis=1, keepdims=True))
    pc = jnp.exp(sc - m)
    pn = jnp.exp(sn - m)
    l = jnp.sum(pc, axis=1, keepdims=True) + jnp.sum(pn, axis=1, keepdims=True)
    acc = _dot_nt(pc.astype(BF16), vc_ref[...].astype(BF16)) + _dot(pn.astype(BF16), vn_ref[...].astype(BF16))
    o_ref[...] = _block_diag_rows(acc / l, N_HEADS_C, n_new)


def _dilated_sample(q, k_new, v_new, cache_k, cache_v):
    db, ds, w = q.shape
    wb = cache_k.shape[1]
    rows = N_HEADS_C * ds
    qh = q.reshape(db, ds, N_HEADS_C, HEAD_DIM).transpose(0, 2, 1, 3)
    eye = jnp.eye(N_HEADS_C, dtype=q.dtype)
    qbd = (qh[:, :, :, None, :] * eye[None, :, None, :, None]).reshape(db, rows, w)
    kt = jnp.transpose(cache_k, (0, 2, 3, 1)).reshape(db, w, wb)
    vt = jnp.transpose(cache_v, (0, 2, 3, 1)).reshape(db, w, wb)
    pad = lambda a: jnp.pad(a, ((0, 0), (0, LANE - ds), (0, 0)))
    rel = (wb + np.arange(ds)[:, None]) - np.arange(wb + LANE)[None, :]
    bias = _mult_bias(rel)
    bias[:, wb + ds:] = NEG
    tbl = np.tile(bias, (N_HEADS_C, 1))
    per_b = lambda *shape: pl.BlockSpec((None,) + shape, lambda b: (b,) + (0,) * len(shape))
    const = lambda a: pl.BlockSpec(a.shape, lambda b: (0,) * a.ndim)
    tc, tn = jnp.asarray(tbl[:, :wb]), jnp.asarray(tbl[:, wb:])
    return pl.pallas_call(
        functools.partial(_dil_sample_kernel, n_new=ds), grid=(db,),
        in_specs=[per_b(rows, w), per_b(w, wb), per_b(LANE, w), per_b(w, wb), per_b(LANE, w), const(tc), const(tn)],
        out_specs=per_b(ds, w),
        out_shape=jax.ShapeDtypeStruct((db, ds, w), F32),
        compiler_params=_cparams(("arbitrary",)), name="dilated_sample",
    )(qbd, kt, pad(k_new), vt, pad(v_new), tc, tn)


def kernel(x_prompt, x_sample, c_prompt, c_sample, cache_a_k, cache_a_v, cache_a_idx_k, state_conv, cache_c_k, cache_c_v, page_table, norm_mix_g, norm_ffn_g, w_ada, b_ada, w_in_even, w_out_even, a_q_norm_g, a_k_norm_g, b_conv_w, b_conv_b, b_ln_g, b_ln_b, w_in_odd, w_out_odd, c_q_norm_g, c_k_norm_g, w_router, b_router, w_gate_e, w_up_e, w_down_e, w_gate_s, w_up_s, w_down_s):
    bp, sp, d = x_prompt.shape
    bs, ss, _ = x_sample.shape
    depth = w_ada.shape[0]
    past = page_table.shape[1] * cache_a_k.shape[2]
    win_buf = cache_c_k.shape[2]
    rp, rs = bp * sp, bs * ss

    tab_p = _rope_tables(jnp.arange(sp, dtype=I32))
    tab_s = _rope_tables(jnp.tile(past + jnp.arange(ss, dtype=I32), bs))
    row = lambda a: a.reshape(1, -1)
    gain2 = lambda g: jnp.tile(g, 2).reshape(1, LANE)

    xp = x_prompt.reshape(rp, d)
    xs = x_sample.reshape(rs, d)
    c_all = jnp.concatenate([c_prompt, c_sample], axis=0)
    c_all = jnp.pad(c_all, ((0, -c_all.shape[0] % 8), (0, 0)))

    ak_p, av_p, ai_p, ak_s, av_s, ai_s, cv_p, cv_s = [], [], [], [], [], [], [], []
    ck_p, cw_p, ck_s, cw_s = [], [], [], []
    for layer in range(depth):
        mod = _adaln(c_all, w_ada[layer], b_ada[layer])
        mod_p = [m.reshape(bp, 1, d) for m in jnp.split(mod[:bp], 6, axis=-1)]
        mod_s = [jnp.repeat(m, ss, axis=0) for m in jnp.split(mod[bp:bp + bs], 6, axis=-1)]
        gmix, gffn = row(norm_mix_g[layer]), row(norm_ffn_g[layer])
        if layer % 2 == 0:
            e = layer // 2
            w_in = _perm_w_even(w_in_even[e])
            w_out = w_out_even[e].astype(BF16)
            qg, kg = gain2(a_q_norm_g[e]), gain2(a_k_norm_g[e])
            outs = [(A_W, BF16), (A_W, F32), (A_W, BF16), (A_W, F32), (A_W, BF16), (2 * LANE, BF16),
                    (LANE, F32), (LANE, BF16), (LANE, F32), (CONV_CH, F32)]
            q, k, kb, v, vb, qi, ki2, ki2b, wi, u = _project(
                _proj_even_kernel, xp, gmix, mod_p[0], mod_p[1], w_in, tab_p, qg, kg, sp, outs)
            b3 = lambda a: a.reshape(bp, sp, a.shape[-1])
            oa = _dsa_prompt(b3(q), b3(qi), b3(wi), b3(ki2b), b3(kb), b3(vb)).reshape(rp, A_W)
            u3 = b3(u)
            ob = _conv_module(u3, jnp.zeros((bp, CONV_WIDTH - 1, CONV_CH), F32),
                              b_conv_w[e], b_conv_b[e], b_ln_g[e], b_ln_b[e]).reshape(rp, CONV_CH)
            lhs_p = [oa, ob]
            ak_p.append(k.reshape(bp, sp, N_HEADS_A, HEAD_DIM))
            av_p.append(v.reshape(bp, sp, N_HEADS_A, HEAD_DIM))
            ai_p.append(ki2[:, :IDX_DIM].reshape(bp, sp, IDX_DIM))
            xpad = jnp.concatenate([jnp.zeros((bp, CONV_WIDTH - 1, CONV_CH), F32), u3], axis=1)
            cv_p.append(xpad[:, -(CONV_WIDTH - 1):])
            q, k, kb, v, vb, qi, ki2, ki2b, wi, u = _project(
                _proj_even_kernel, xs, gmix, mod_s[0], mod_s[1], w_in, tab_s, qg, kg, rs, outs)
            s3 = lambda a: a.reshape(bs, ss, a.shape[-1])
            oa = _dsa_sample(s3(q), s3(k), s3(v), s3(qi), s3(ki2)[:, :, :IDX_DIM], s3(wi)[:, :, :N_IDX_HEADS],
                             cache_a_k[e], cache_a_v[e], cache_a_idx_k[e], page_table)
            u3 = s3(u)
            ob = _conv_module(u3, state_conv[e], b_conv_w[e], b_conv_b[e], b_ln_g[e], b_ln_b[e])
            lhs_s = [oa.reshape(rs, A_W).astype(BF16), ob.reshape(rs, CONV_CH)]
            ak_s.append(k.reshape(bs, ss, N_HEADS_A, HEAD_DIM))
            av_s.append(v.reshape(bs, ss, N_HEADS_A, HEAD_DIM))
            ai_s.append(ki2[:, :IDX_DIM].reshape(bs, ss, IDX_DIM))
            cv_s.append(jnp.concatenate([state_conv[e], u3], axis=1)[:, -(CONV_WIDTH - 1):])
            ws = [w_out[:A_W], w_out[A_W:]]
        else:
            o = layer // 2
            w_in = w_in_odd[o].astype(BF16)
            qg, kg = gain2(c_q_norm_g[o]), gain2(c_k_norm_g[o])
            outs = [(C_W, BF16), (C_W, F32), (C_W, BF16), (C_W, F32), (C_W, BF16)]
            q, k, kb, v, vb = _project(_proj_odd_kernel, xp, gmix, mod_p[0], mod_p[1], w_in, tab_p, qg, kg, sp, outs)
            b3 = lambda a: a.reshape(bp, sp, a.shape[-1])
            lhs_p = [_dilated_prompt(b3(q), b3(kb), b3(vb)).reshape(rp, C_W)]
            k4, v4 = k.reshape(bp, sp, N_HEADS_C, HEAD_DIM), v.reshape(bp, sp, N_HEADS_C, HEAD_DIM)
            padt = max(win_buf - sp, 0)
            tail = lambda a: jnp.pad(a, ((0, 0), (padt, 0), (0, 0), (0, 0)))[:, -win_buf:]
            ck_p.append(tail(k4))
            cw_p.append(tail(v4))
            q, k, kb, v, vb = _project(_proj_odd_kernel, xs, gmix, mod_s[0], mod_s[1], w_in, tab_s, qg, kg, rs, outs)
            s3 = lambda a: a.reshape(bs, ss, a.shape[-1])
            att = _dilated_sample(s3(q), s3(k), s3(v), cache_c_k[o], cache_c_v[o])
            lhs_s = [att.reshape(rs, C_W).astype(BF16)]
            k4, v4 = k.reshape(bs, ss, N_HEADS_C, HEAD_DIM), v.reshape(bs, ss, N_HEADS_C, HEAD_DIM)
            ck_s.append(jnp.concatenate([cache_c_k[o], k4], axis=1)[:, -win_buf:])
            cw_s.append(jnp.concatenate([cache_c_v[o], v4], axis=1)[:, -win_buf:])
            ws = [w_out_odd[o].astype(BF16)]

        wg, wu, wd = w_gate_e[layer].astype(BF16), w_up_e[layer].astype(BF16), w_down_e[layer].astype(BF16)
        wgs, wus, wds = w_gate_s[layer].astype(BF16), w_up_s[layer].astype(BF16), w_down_s[layer].astype(BF16)
        xn, h, idx, gate = _outproj_router(lhs_p, ws, xp, mod_p[2], gffn, mod_p[3], mod_p[4],
                                           w_router[layer], b_router[layer], sp)
        xp = _moe(h, xn, mod_p[5], idx[:, :TOP_K], gate[:, :TOP_K], wg, wu, wd, wgs, wus, wds, sp)
        xn, h, idx, gate = _outproj_router(lhs_s, ws, xs, mod_s[2], gffn, mod_s[3], mod_s[4],
                                           w_router[layer], b_router[layer], rs)
        xs = _moe(h, xn, mod_s[5], idx[:, :TOP_K], gate[:, :TOP_K], wg, wu, wd, wgs, wus, wds, rs)

    return (xp.reshape(bp, sp, d), xs.reshape(bs, ss, d),
            jnp.stack(ak_p), jnp.stack(av_p), jnp.stack(ai_p),
            jnp.stack(ak_s), jnp.stack(av_s), jnp.stack(ai_s),
            jnp.stack(cv_p), jnp.stack(cv_s),
            jnp.stack(ck_p), jnp.stack(cw_p), jnp.stack(ck_s), jnp.stack(cw_s))
```

```python
import functools

import numpy as np
import jax
import jax.numpy as jnp
from jax import lax
from jax.experimental import pallas as pl
from jax.experimental.pallas import tpu as pltpu

F32, BF16, I32 = jnp.float32, jnp.bfloat16, jnp.int32

HEAD_DIM = 64
N_HEADS_A = 8
CONV_CH = 512
CONV_WIDTH = 31
N_IDX_HEADS = 4
IDX_DIM = 64
TOPK_MAX = 256
N_HEADS_C = 16
DILATED_PATTERNS = ((128, 1), (512, 4), (2048, 16))
N_EXPERTS = 64
TOP_K = 6
ROUTED_SCALE = 2.5
ROPE_THETA = 10000.0
EPS = 1e-6

LANE = 128
ROW_GROUP = 8
DSA_Q_TILE = 256
DIL_Q_TILE = 512
MOE_TOKEN_TILE = 2048
MOE_ROW_CHUNK = 256
NEG = -1e30
KEY_NEG_INF = -2139095041
VMEM_LIMIT = 60 * 1024 * 1024

A_W = N_HEADS_A * HEAD_DIM
C_W = N_HEADS_C * HEAD_DIM


def _cparams(sem):
    return pltpu.CompilerParams(dimension_semantics=sem, vmem_limit_bytes=VMEM_LIMIT)


def _dot(a, b):
    return jnp.dot(a, b, preferred_element_type=F32)


def _dot_nt(a, b):
    return lax.dot_general(a, b, (((1,), (1,)), ((), ())), preferred_element_type=F32)


def _dot_tn(a, b):
    return lax.dot_general(a, b, (((0,), (0,)), ((), ())), preferred_element_type=F32)


def _silu(x):
    return x * jax.nn.sigmoid(x)


def _rms_mod(x, g, shift, scale):
    ms = jnp.mean(x * x, axis=-1, keepdims=True)
    y = x * lax.rsqrt(ms + EPS) * g
    return y * (1.0 + scale) + shift


def _to_key(x):
    b = lax.bitcast_convert_type(x, I32)
    k = jnp.where(b < 0, b ^ 0x7FFFFFFF, b)
    return jnp.where(k == -1, 0, k)


def _rope_tables(pos):
    half = HEAD_DIM // 2
    inv = ROPE_THETA ** (-jnp.arange(half, dtype=F32) / half)
    ang = pos.astype(F32)[:, None] * inv[None, :]
    c, s = jnp.cos(ang), jnp.sin(ang)
    z = jnp.zeros_like(s)
    cos = jnp.tile(c, (1, 4))
    sin_a = jnp.tile(jnp.concatenate([z, s], axis=1), (1, 2))
    sin_b = jnp.tile(jnp.concatenate([-s, z], axis=1), (1, 2))
    return cos, sin_a, sin_b


def _head_sum_matrix():
    r = np.arange(LANE)
    return jnp.asarray((r[:, None] // HEAD_DIM == r[None, :] // HEAD_DIM).astype(np.float32), BF16)


def _prefix_matrix(n):
    r = np.arange(n)
    return jnp.asarray((r[:, None] <= r[None, :]).astype(np.float32), BF16)


def _mult_bias(rel):
    mult = np.zeros(rel.shape, np.int64)
    for w, d in DILATED_PATTERNS:
        mult += ((rel >= 0) & (rel <= w) & (rel % d == 0))
    return np.where(mult > 0, np.log(np.maximum(mult, 1)), NEG).astype(np.float32)


def _ada_kernel(c_ref, w_ref, b_ref, o_ref):
    a = _silu(c_ref[...]).astype(BF16)
    o_ref[...] = _dot(a, w_ref[...].astype(BF16)) + b_ref[...]


def _adaln(c, w, b):
    m, d = c.shape
    n = w.shape[1]
    tn = 1536
    return pl.pallas_call(
        _ada_kernel, grid=(n // tn,),
        in_specs=[pl.BlockSpec((m, d), lambda j: (0, 0)),
                  pl.BlockSpec((d, tn), lambda j: (0, j)),
                  pl.BlockSpec((1, tn), lambda j: (0, j))],
        out_specs=pl.BlockSpec((m, tn), lambda j: (0, j)),
        out_shape=jax.ShapeDtypeStruct((m, n), F32),
        compiler_params=_cparams(("arbitrary",)), name="adaln")(c, w, b.reshape(1, n))


def _norm_rope(z, bsum, g, cos, sin_a, sin_b, norm):
    if norm:
        sq = z * z
        hi = sq.astype(BF16)
        lo = (sq - hi.astype(F32)).astype(BF16)
        ms = (_dot(hi, bsum) + _dot(lo, bsum)) * (1.0 / HEAD_DIM)
        z = z * lax.rsqrt(ms + EPS) * g
    return z * cos + pltpu.roll(z, 32, 1) * sin_a + pltpu.roll(z, 96, 1) * sin_b


def _proj_even_kernel(x_ref, gn_ref, sh_ref, sc_ref, w_ref, cos_ref, sa_ref, sb_ref, qg_ref, kg_ref, bsum_ref,
                      q_ref, k_ref, kb_ref, v_ref, vb_ref, qi_ref, ki_ref, kib_ref, wi_ref, u_ref, h_scr):
    h_scr[...] = _rms_mod(x_ref[...], gn_ref[...], sh_ref[...], sc_ref[...]).astype(BF16)
    hb = h_scr[...]
    cos, sa, sb, bsum = cos_ref[...], sa_ref[...], sb_ref[...], bsum_ref[...]

    def seg(lo, n):
        return _dot(hb, w_ref[:, lo:lo + n])

    z = seg(0, A_W)
    for c in range(A_W // LANE):
        sl = slice(c * LANE, (c + 1) * LANE)
        q_ref[:, sl] = _norm_rope(z[:, sl], bsum, qg_ref[...], cos, sa, sb, True).astype(BF16)
    z = seg(A_W, A_W)
    for c in range(A_W // LANE):
        sl = slice(c * LANE, (c + 1) * LANE)
        kk = _norm_rope(z[:, sl], bsum, kg_ref[...], cos, sa, sb, True)
        k_ref[:, sl] = kk
        kb_ref[:, sl] = kk.astype(BF16)
    z = seg(2 * A_W, A_W)
    v_ref[...] = z
    vb_ref[...] = z.astype(BF16)
    z = seg(3 * A_W, 2 * LANE)
    for c in range(2):
        sl = slice(c * LANE, (c + 1) * LANE)
        qi_ref[:, sl] = _norm_rope(z[:, sl], bsum, None, cos, sa, sb, False).astype(BF16)
    z = _norm_rope(seg(3 * A_W + 2 * LANE, LANE), bsum, None, cos, sa, sb, False)
    ki_ref[...] = z
    kib_ref[...] = z.astype(BF16)
    off = 3 * A_W + 3 * LANE
    ua = seg(off, CONV_CH)
    ub = seg(off + CONV_CH, CONV_CH)
    u_ref[...] = ua * jax.nn.sigmoid(ub)
    wi_ref[...] = seg(off + 2 * CONV_CH, LANE) * (N_IDX_HEADS ** -0.5 * IDX_DIM ** -0.5)


def _proj_odd_kernel(x_ref, gn_ref, sh_ref, sc_ref, w_ref, cos_ref, sa_ref, sb_ref, qg_ref, kg_ref, bsum_ref,
                     q_ref, k_ref, kb_ref, v_ref, vb_ref, h_scr):
    h_scr[...] = _rms_mod(x_ref[...], gn_ref[...], sh_ref[...], sc_ref[...]).astype(BF16)
    hb = h_scr[...]
    cos, sa, sb, bsum = cos_ref[...], sa_ref[...], sb_ref[...], bsum_ref[...]
    for half in range(2):
        z = _dot(hb, w_ref[:, half * A_W:(half + 1) * A_W])
        for c in range(A_W // LANE):
            sl = slice(c * LANE, (c + 1) * LANE)
            dl = slice(half * A_W + c * LANE, half * A_W + (c + 1) * LANE)
            q_ref[:, dl] = _norm_rope(z[:, sl], bsum, qg_ref[...], cos, sa, sb, True).astype(BF16)
    for half in range(2):
        z = _dot(hb, w_ref[:, C_W + half * A_W:C_W + (half + 1) * A_W])
        for c in range(A_W // LANE):
            sl = slice(c * LANE, (c + 1) * LANE)
            dl = slice(half * A_W + c * LANE, half * A_W + (c + 1) * LANE)
            kk = _norm_rope(z[:, sl], bsum, kg_ref[...], cos, sa, sb, True)
            k_ref[:, dl] = kk
            kb_ref[:, dl] = kk.astype(BF16)
    for half in range(2):
        z = _dot(hb, w_ref[:, 2 * C_W + half * A_W:2 * C_W + (half + 1) * A_W])
        v_ref[:, half * A_W:(half + 1) * A_W] = z
        vb_ref[:, half * A_W:(half + 1) * A_W] = z.astype(BF16)


def _mod_spec(mod, tr, tiles_per_group):
    d = mod.shape[-1]
    if mod.ndim == 3:
        return pl.BlockSpec((None, 1, d), lambda i: (i // tiles_per_group, 0, 0))
    return pl.BlockSpec((tr, d), lambda i: (i, 0))


def _row_tile(rows, want):
    return want if rows % want == 0 else rows


def _project(kern, x, gn, shift, scale, w, tables, qg, kg, group_rows, outs):
    r, d = x.shape
    tr = _row_tile(min(r, group_rows), 512)
    tpg = max(group_rows // tr, 1)
    tpt = tables[0].shape[0] // tr
    const = lambda a: pl.BlockSpec(a.shape, lambda i: (0,) * a.ndim)
    tab = pl.BlockSpec((tr, LANE), lambda i: (i % tpt, 0))
    bsum = _head_sum_matrix()
    return pl.pallas_call(
        kern, grid=(r // tr,),
        in_specs=[pl.BlockSpec((tr, d), lambda i: (i, 0)), const(gn), _mod_spec(shift, tr, tpg),
                  _mod_spec(scale, tr, tpg), const(w), tab, tab, tab, const(qg), const(kg), const(bsum)],
        out_specs=[pl.BlockSpec((tr, wd), lambda i: (i, 0)) for wd, _ in outs],
        out_shape=[jax.ShapeDtypeStruct((r, wd), dt) for wd, dt in outs],
        scratch_shapes=[pltpu.VMEM((tr, d), BF16)],
        compiler_params=_cparams(("arbitrary",)), name=kern.__name__.strip("_"),
    )(x, gn, shift, scale, w, *tables, qg, kg, bsum)


def _perm_w_even(w):
    d = w.shape[0]
    cuts = np.cumsum([A_W, A_W, A_W, N_IDX_HEADS * IDX_DIM, IDX_DIM, N_IDX_HEADS, CONV_CH])
    q, k, v, qi, ki, wi, ua, ub = jnp.split(w, [int(c) for c in cuts], axis=1)
    pad = jnp.zeros((d, LANE - N_IDX_HEADS), w.dtype)
    return jnp.concatenate([q, k, v, qi, ki, ki, ua, ub, wi, pad], axis=1).astype(BF16)


def _dsa_prompt_kernel(q_ref, qi_ref, wi_ref, ki_ref, k_ref, v_ref, u_ref, o_ref, key_ref, bias_ref,
                       qs_ref, m_ref, l_ref, acc_ref, *, tq, kc, topk):
    i = pl.program_id(1)
    q0 = i * tq
    cd = q0 // kc
    lane = lax.broadcasted_iota(I32, (tq, LANE), 1)
    lo_half = lane < HEAD_DIM
    zero = jnp.zeros((tq, LANE), BF16)

    qi = qi_ref[...]
    qm = [jnp.where(lo_half if h % 2 == 0 else ~lo_half, qi[:, (h // 2) * LANE:(h // 2 + 1) * LANE], zero)
          for h in range(N_IDX_HEADS)]
    wi = wi_ref[...]
    wcol = [wi[:, h:h + 1] for h in range(N_IDX_HEADS)]

    def score_chunk(c):
        kk = ki_ref[pl.ds(pl.multiple_of(c * kc, kc), kc), :]
        acc = None
        for h in range(N_IDX_HEADS):
            t = wcol[h] * jnp.maximum(_dot_nt(qm[h], kk), 0.0)
            acc = t if acc is None else acc + t
        return acc

    def fill(c, _):
        key_ref[c] = _to_key(score_chunk(c))
        return 0

    lax.fori_loop(0, cd, fill, 0)
    kpos = cd * kc + lax.broadcasted_iota(I32, (tq, kc), 1)
    qpos = q0 + lax.broadcasted_iota(I32, (tq, kc), 0)
    key_ref[cd] = _to_key(jnp.where(kpos <= qpos, score_chunk(cd), -jnp.inf))

    def count(cand, strict):
        def body(c, cnt):
            kk = key_ref[c]
            one = jnp.where((kk > cand) if strict else (kk >= cand), 1.0, 0.0)
            part = one[:, 0:LANE]
            for j in range(1, kc // LANE):
                part = part + one[:, j * LANE:(j + 1) * LANE]
            return cnt + part
        cnt = lax.fori_loop(0, cd + 1, body, jnp.zeros((tq, LANE), F32))
        return jnp.sum(cnt, axis=1, keepdims=True)

    kf = float(topk)
    thr = jnp.where(count(jnp.zeros((tq, 1), I32), False) >= kf, 0, -2 ** 31).astype(I32)

    def bit(b, thr):
        cand = thr + jnp.left_shift(jnp.int32(1), 30 - b)
        return jnp.where(count(cand, False) >= kf, cand, thr)

    thr = lax.fori_loop(0, 31, bit, thr)
    need = kf - count(thr, True)
    upper = u_ref[...]

    def select(c, run):
        kk = key_ref[c]
        tie = kk == thr
        pre = _dot(jnp.where(tie, 1.0, 0.0).astype(BF16), upper) + run
        sel = ((kk > thr) | (tie & (pre <= need))) & (kk > KEY_NEG_INF)
        bias_ref[c] = jnp.where(sel, 0.0, NEG)
        return pre[:, kc - 1:kc]

    lax.fori_loop(0, cd + 1, select, jnp.zeros((tq, 1), F32))

    n_pairs = A_W // LANE
    for hp in range(n_pairs):
        qs_ref[hp] = _stack_pair(q_ref[:, hp * LANE:(hp + 1) * LANE], lo_half)
    _flash_init(m_ref, l_ref, acc_ref)

    def attend(c, _):
        rows = pl.ds(pl.multiple_of(c * kc, kc), kc)
        b = bias_ref[c]
        bias2 = jnp.concatenate([b, b], axis=0)
        for hp in range(n_pairs):
            cols = slice(hp * LANE, (hp + 1) * LANE)
            _flash_step(qs_ref[hp], k_ref[rows, cols], v_ref[rows, cols], bias2, m_ref, l_ref, acc_ref, hp)
        return 0

    lax.fori_loop(0, cd + 1, attend, 0)
    for hp in range(n_pairs):
        o_ref[:, hp * LANE:(hp + 1) * LANE] = _flash_finish(l_ref, acc_ref, hp, lo_half, tq).astype(o_ref.dtype)


def _stack_pair(qp, lo_half):
    qsc = (qp.astype(F32) * (HEAD_DIM ** -0.5)).astype(BF16)
    zero = jnp.zeros_like(qsc)
    return jnp.concatenate([jnp.where(lo_half, qsc, zero), jnp.where(lo_half, zero, qsc)], axis=0)


def _flash_init(m_ref, l_ref, acc_ref):
    m_ref[...] = jnp.full(m_ref.shape, NEG, F32)
    l_ref[...] = jnp.zeros(l_ref.shape, F32)
    acc_ref[...] = jnp.zeros(acc_ref.shape, F32)


def _flash_step(qs, kblk, vblk, bias2, m_ref, l_ref, acc_ref, idx):
    s = _dot_nt(qs, kblk) + bias2
    m_old = m_ref[idx]
    mn = jnp.maximum(m_old, jnp.max(s, axis=1, keepdims=True))
    p = jnp.exp(s - jnp.tile(mn, (1, s.shape[1] // LANE)))
    a = jnp.exp(m_old - mn)
    l_ref[idx] = a * l_ref[idx] + jnp.sum(p, axis=1, keepdims=True)
    acc_ref[idx] = a * acc_ref[idx] + _dot(p.astype(BF16), vblk)
    m_ref[idx] = mn


def _flash_finish(l_ref, acc_ref, idx, lo_half, t):
    o = acc_ref[idx] / l_ref[idx]
    return jnp.where(lo_half, o[:t], o[t:])


def _dsa_prompt(q, qi, wi, ki2, k, v):
    b, s, _ = q.shape
    tq = min(DSA_Q_TILE, s)
    kc = min(512, s)
    topk = min(TOPK_MAX, s // 4)
    nc = s // kc
    upper = _prefix_matrix(kc)
    qspec = lambda w: pl.BlockSpec((None, tq, w), lambda bi, i: (bi, i, 0))
    kspec = lambda w: pl.BlockSpec((None, s, w), lambda bi, i: (bi, 0, 0), pipeline_mode=pl.Buffered(1))
    return pl.pallas_call(
        functools.partial(_dsa_prompt_kernel, tq=tq, kc=kc, topk=topk),
        grid=(b, s // tq),
        in_specs=[qspec(A_W), qspec(2 * LANE), qspec(LANE), kspec(LANE), kspec(A_W), kspec(A_W),
                  pl.BlockSpec((kc, kc), lambda bi, i: (0, 0))],
        out_specs=qspec(A_W),
        out_shape=jax.ShapeDtypeStruct((b, s, A_W), BF16),
        scratch_shapes=[pltpu.VMEM((nc, tq, kc), I32), pltpu.VMEM((nc, tq, kc), F32),
                        pltpu.VMEM((A_W // LANE, 2 * tq, LANE), BF16)]
        + [pltpu.VMEM((A_W // LANE, 2 * tq, LANE), F32)] * 3,
        compiler_params=_cparams(("arbitrary", "arbitrary")), name="dsa_prompt",
    )(q, qi, wi, ki2, k, v, upper)


def _dsa_sample_select_kernel(pt_ref, qi_ref, wi_ref, *rest, n_new, npg, pg, topk):
    del pt_ref
    kip_refs = rest[:pg]
    kin_ref, u_ref, o_ref, key_ref = rest[pg:]
    p = pl.program_id(1)
    qi = qi_ref[...]
    wi = wi_ref[...]
    n_chunks = key_ref.shape[0]

    def score(dots):
        r = jnp.maximum(dots, 0.0) * wi
        acc = r[0:n_new]
        for h in range(1, N_IDX_HEADS):
            acc = acc + r[h * n_new:(h + 1) * n_new]
        return acc

    pages = jnp.concatenate([r[...] for r in kip_refs], axis=1).astype(BF16)
    dots = _dot(qi, pages)
    for g in range(pg):
        key_ref[p * pg + g] = _to_key(score(dots[:, g * LANE:(g + 1) * LANE]))

    @pl.when(p == 0)
    def _():
        for c in range(npg + 1, n_chunks):
            key_ref[c] = jnp.full((n_new, LANE), -2 ** 31, I32)

    @pl.when(p == npg // pg - 1)
    def _():
        col = lax.broadcasted_iota(I32, (n_new, LANE), 1)
        row = lax.broadcasted_iota(I32, (n_new, LANE), 0)
        new = score(_dot_nt(qi, kin_ref[...].astype(BF16)))
        key_ref[npg] = _to_key(jnp.where(col <= row, new, -jnp.inf))

        def count(cand, strict):
            def body(c, cnt):
                kk = key_ref[pl.ds(pl.multiple_of(c * ROW_GROUP, ROW_GROUP), ROW_GROUP)]
                hit = jnp.where((kk > cand) if strict else (kk >= cand), 1.0, 0.0)
                return cnt + jnp.sum(hit, axis=0)
            cnt = lax.fori_loop(0, n_chunks // ROW_GROUP, body, jnp.zeros((n_new, LANE), F32), unroll=True)
            return jnp.sum(cnt, axis=1, keepdims=True)

        kf = float(topk)
        thr = jnp.where(count(jnp.zeros((n_new, 1), I32), False) >= kf, 0, -2 ** 31).astype(I32)

        def bit(b, thr):
            cand = thr + jnp.left_shift(jnp.int32(1), 30 - b)
            return jnp.where(count(cand, False) >= kf, cand, thr)

        thr = lax.fori_loop(0, 31, bit, thr)
        need = kf - count(thr, True)
        upper = u_ref[...]

        def select(g, run):
            c0 = pl.multiple_of(g * ROW_GROUP, ROW_GROUP)
            kk = key_ref[pl.ds(c0, ROW_GROUP)]
            tie = kk == thr
            ties = jnp.where(tie, 1.0, 0.0).reshape(ROW_GROUP * n_new, LANE).astype(BF16)
            pre = _dot(ties, upper).reshape(ROW_GROUP, n_new, LANE)
            totals = [pre[c][:, LANE - 1:LANE] for c in range(ROW_GROUP)]
            for c in range(ROW_GROUP):
                sel = ((kk[c] > thr) | (tie[c] & (pre[c] + run <= need))) & (kk[c] > KEY_NEG_INF)
                o_ref[c0 + c] = jnp.where(sel, 1.0, 0.0)
                run = run + totals[c]
            return run

        lax.fori_loop(0, n_chunks // ROW_GROUP, select, jnp.zeros((n_new, 1), F32))


def _block_diag_rows(o_full, n_heads, n_new):
    head_of_lane = lax.broadcasted_iota(I32, (n_new, o_full.shape[1]), 1) // HEAD_DIM
    out = jnp.zeros((n_new, o_full.shape[1]), F32)
    for h in range(n_heads):
        out = jnp.where(head_of_lane == h, o_full[h * n_new:(h + 1) * n_new], out)
    return out


def _dsa_sample_attend_kernel(pt_ref, qbd_ref, *rest, npg, pg, n_new):
    del pt_ref
    kp_refs, vp_refs = rest[:pg], rest[pg:2 * pg]
    kn_ref, vn_ref, sel_ref, o_ref, m_ref, l_ref, acc_ref = rest[2 * pg:]
    p = pl.program_id(1)
    qbd = qbd_ref[...]

    @pl.when(p == 0)
    def _():
        _flash_init(m_ref, l_ref, acc_ref)

    def bias(sel):
        return (jnp.concatenate([sel] * N_HEADS_A, axis=0) - 1.0) * (-NEG)

    def update(scores, pv):
        m = m_ref[...]
        mn = m
        for s in scores:
            mn = jnp.maximum(mn, jnp.max(s, axis=1, keepdims=True))
        a = jnp.exp(m - mn)
        l = a * l_ref[...]
        acc = jnp.tile(a, (1, A_W // LANE)) * acc_ref[...]
        for s, f in zip(scores, pv):
            pr = jnp.exp(s - mn)
            l = l + jnp.sum(pr, axis=1, keepdims=True)
            acc = acc + f(pr.astype(BF16))
        m_ref[...] = mn
        l_ref[...] = l
        acc_ref[...] = acc

    scale = HEAD_DIM ** -0.5
    update([_dot(qbd, kp_refs[g][...].astype(BF16)) * scale + bias(sel_ref[p * pg + g]) for g in range(pg)],
           [lambda pr, g=g: _dot_nt(pr, vp_refs[g][...].astype(BF16)) for g in range(pg)])

    @pl.when(p == npg // pg - 1)
    def _():
        update([_dot_nt(qbd, kn_ref[...].astype(BF16)) * scale + bias(sel_ref[npg])],
               [lambda pr: _dot(pr, vn_ref[...].astype(BF16))])
        o_full = acc_ref[...] / jnp.tile(l_ref[...], (1, A_W // LANE))
        o_ref[...] = _block_diag_rows(o_full, N_HEADS_A, n_new)


def _dsa_sample(q, k_new, v_new, qi, ki_new, wi, pool_k, pool_v, pool_ki, page_table):
    db, ds, _ = q.shape
    n_phys, page = pool_k.shape[0], pool_k.shape[1]
    npg = page_table.shape[1]
    topk = min(TOPK_MAX, (npg * page + ds) // 4)
    pt = page_table.reshape(-1).astype(I32)
    hq = N_IDX_HEADS * ds
    qi_r = qi.reshape(db, ds, N_IDX_HEADS, IDX_DIM).transpose(0, 2, 1, 3).reshape(db, hq, IDX_DIM)
    wi_r = jnp.broadcast_to(wi.transpose(0, 2, 1).reshape(db, hq, 1), (db, hq, LANE))
    pad = lambda a: jnp.pad(a, ((0, 0), (0, page - ds), (0, 0)))
    per_b = lambda *shape: pl.BlockSpec((None,) + shape, lambda b, p, pt: (b,) + (0,) * len(shape))
    pg = next(g for g in (8, 4, 2, 1) if npg % g == 0)
    paged = lambda w: [pl.BlockSpec((None, w, page), lambda b, p, pt, g=g: (pt[b * npg + p * pg + g], 0, 0))
                       for g in range(pg)]
    pool_kt = jnp.transpose(pool_k, (0, 2, 3, 1)).reshape(n_phys, A_W, page)
    pool_vt = jnp.transpose(pool_v, (0, 2, 3, 1)).reshape(n_phys, A_W, page)
    pool_kit = jnp.transpose(pool_ki, (0, 2, 1))
    n_chunks = -(-(npg + 1) // ROW_GROUP) * ROW_GROUP
    sel = pl.pallas_call(
        functools.partial(_dsa_sample_select_kernel, n_new=ds, npg=npg, pg=pg, topk=topk),
        grid_spec=pltpu.PrefetchScalarGridSpec(
            num_scalar_prefetch=1, grid=(db, npg // pg),
            in_specs=[per_b(hq, IDX_DIM), per_b(hq, LANE)] + paged(IDX_DIM)
            + [per_b(page, IDX_DIM), pl.BlockSpec((page, page), lambda b, p, pt: (0, 0))],
            out_specs=per_b(n_chunks, ds, page),
            scratch_shapes=[pltpu.VMEM((n_chunks, ds, page), I32)]),
        out_shape=jax.ShapeDtypeStruct((db, n_chunks, ds, page), F32),
        compiler_params=_cparams(("arbitrary", "arbitrary")), name="dsa_sample_select",
    )(pt, qi_r, wi_r, *([pool_kit] * pg), pad(ki_new), _prefix_matrix(page))

    rows = N_HEADS_A * ds
    qh = q.reshape(db, ds, N_HEADS_A, HEAD_DIM).transpose(0, 2, 1, 3)
    eye = jnp.eye(N_HEADS_A, dtype=q.dtype)
    qbd = (qh[:, :, :, None, :] * eye[None, :, None, :, None]).reshape(db, rows, A_W)
    state = pltpu.VMEM((rows, LANE), F32)
    return pl.pallas_call(
        functools.partial(_dsa_sample_attend_kernel, npg=npg, pg=pg, n_new=ds),
        grid_spec=pltpu.PrefetchScalarGridSpec(
            num_scalar_prefetch=1, grid=(db, npg // pg),
            in_specs=[per_b(rows, A_W)] + paged(A_W) + paged(A_W)
            + [per_b(page, A_W), per_b(page, A_W), per_b(n_chunks, ds, page)],
            out_specs=per_b(ds, A_W),
            scratch_shapes=[state, state, pltpu.VMEM((rows, A_W), F32)]),
        out_shape=jax.ShapeDtypeStruct((db, ds, A_W), F32),
        compiler_params=_cparams(("arbitrary", "arbitrary")), name="dsa_sample_attend",
    )(pt, qbd, *([pool_kt] * pg), *([pool_vt] * pg), pad(k_new), pad(v_new), sel)


def _conv_kernel(u_ref, halo_ref, buf_ref, w_ref, b_ref, g_ref, bb_ref, o_ref, xp_ref, *, tt):
    i = pl.program_id(1)

    @pl.when(i == 0)
    def _():
        xp_ref[0:32, :] = buf_ref[...]

    @pl.when(i > 0)
    def _():
        xp_ref[0:32, :] = halo_ref[...]

    xp_ref[32:32 + tt, :] = u_ref[...]
    w = w_ref[...]
    acc = jnp.zeros((tt, CONV_CH), F32) + b_ref[...]
    for j in range(CONV_WIDTH):
        acc = acc + xp_ref[2 + j:2 + j + tt, :] * w[j:j + 1, :]
    mu = jnp.mean(acc, axis=-1, keepdims=True)
    var = jnp.mean(jnp.square(acc - mu), axis=-1, keepdims=True)
    yn = (acc - mu) * lax.rsqrt(var + EPS) * g_ref[...] + bb_ref[...]
    o_ref[...] = _silu(yn).astype(o_ref.dtype)


def _conv_module(u, buf30, conv_w, conv_b, ln_g, ln_b):
    b, t, c = u.shape
    tt = _row_tile(t, 512)
    buf = jnp.pad(buf30, ((0, 0), (2, 0), (0, 0)))
    halo_src = u if t >= 32 else buf
    hb = tt // 32
    w = jnp.pad(conv_w, ((0, 32 - CONV_WIDTH), (0, 0)))
    const = lambda a: pl.BlockSpec(a.shape, lambda bi, i: (0,) * a.ndim)
    row = lambda a: a.reshape(1, c)
    return pl.pallas_call(
        functools.partial(_conv_kernel, tt=tt), grid=(b, t // tt),
        in_specs=[pl.BlockSpec((None, tt, c), lambda bi, i: (bi, i, 0)),
                  pl.BlockSpec((None, 32, c), lambda bi, i: (bi, jnp.maximum(i * hb - 1, 0), 0)),
                  pl.BlockSpec((None, 32, c), lambda bi, i: (bi, 0, 0)),
                  const(w), const(row(conv_b)), const(row(ln_g)), const(row(ln_b))],
        out_specs=pl.BlockSpec((None, tt, c), lambda bi, i: (bi, i, 0)),
        out_shape=jax.ShapeDtypeStruct((b, t, c), BF16),
        scratch_shapes=[pltpu.VMEM((32 + tt, c), F32)],
        compiler_params=_cparams(("arbitrary", "arbitrary")), name="conv_module",
    )(u, halo_src, buf, w, row(conv_b), row(ln_g), row(ln_b))


def _outproj_kernel(*refs, n_lhs):
    lhs, ws = refs[:n_lhs], refs[n_lhs:2 * n_lhs]
    x_ref, g2_ref, gn_ref, sh_ref, sc_ref, wr_ref, br_ref, xn_ref, h_ref, idx_ref, gate_ref = refs[2 * n_lhs:]
    mix = _dot(lhs[0][...], ws[0][...])
    for a, w in zip(lhs[1:], ws[1:]):
        mix = mix + _dot(a[...], w[...])
    xn = x_ref[...] + g2_ref[...] * mix
    xn_ref[...] = xn
    h = _rms_mod(xn, gn_ref[...], sh_ref[...], sc_ref[...])
    h_ref[...] = h
    s = jax.nn.sigmoid(_dot(h.astype(BF16), wr_ref[...]))
    tr = s.shape[0]
    lane = lax.broadcasted_iota(I32, (tr, LANE), 1)
    lanef = lane.astype(F32)
    vals = jnp.where(lane < N_EXPERTS, s + br_ref[...], -jnp.inf)
    idxm = jnp.zeros((tr, LANE), F32)
    gm = jnp.zeros((tr, LANE), F32)
    for j in range(TOP_K):
        m = jnp.max(vals, axis=1, keepdims=True)
        ix = jnp.min(jnp.where(vals == m, lanef, float(LANE)), axis=1, keepdims=True)
        hit = lanef == ix
        sj = jnp.sum(jnp.where(hit, s, 0.0), axis=1, keepdims=True)
        idxm = jnp.where(lane == j, ix, idxm)
        gm = jnp.where(lane == j, sj, gm)
        vals = jnp.where(hit, -jnp.inf, vals)
    idx_ref[...] = idxm.astype(I32)
    gate_ref[...] = gm / jnp.sum(gm, axis=1, keepdims=True) * ROUTED_SCALE


def _outproj_router(lhs, ws, x, gate2, gn, shift, scale, w_router, b_router, group_rows):
    r, d = x.shape
    tr = _row_tile(min(r, group_rows), 512)
    tpg = max(group_rows // tr, 1)
    const = lambda a: pl.BlockSpec(a.shape, lambda i: (0,) * a.ndim)
    rowblk = lambda w: pl.BlockSpec((tr, w), lambda i: (i, 0))
    wr = jnp.pad(w_router, ((0, 0), (0, LANE - N_EXPERTS))).astype(BF16)
    br = jnp.pad(b_router, (0, LANE - N_EXPERTS)).reshape(1, LANE)
    return pl.pallas_call(
        functools.partial(_outproj_kernel, n_lhs=len(lhs)), grid=(r // tr,),
        in_specs=[rowblk(a.shape[1]) for a in lhs] + [const(w) for w in ws]
        + [rowblk(d), _mod_spec(gate2, tr, tpg), const(gn), _mod_spec(shift, tr, tpg), _mod_spec(scale, tr, tpg),
           const(wr), const(br)],
        out_specs=[rowblk(d), rowblk(d), rowblk(LANE), rowblk(LANE)],
        out_shape=[jax.ShapeDtypeStruct((r, d), F32), jax.ShapeDtypeStruct((r, d), F32),
                   jax.ShapeDtypeStruct((r, LANE), I32), jax.ShapeDtypeStruct((r, LANE), F32)],
        compiler_params=_cparams(("arbitrary",)), name="outproj_router",
    )(*lhs, *ws, x, gate2, gn, shift, scale, wr, br)


def _moe_kernel(cnt_ref, off_ref, tok_ref, gate_ref, h_ref, x_ref, g5_ref, wg_ref, wu_ref, wd_ref,
                wgs_ref, wus_ref, wds_ref, o_ref, acc_ref, xg_ref, y_ref, *, t_rows, ch):
    t = pl.program_id(0)
    e = pl.program_id(1)
    ne = pl.num_programs(1)

    def ffn(xb, wg, wu, wd):
        a = _silu(_dot(xb, wg)) * _dot(xb, wu)
        return _dot(a.astype(BF16), wd)

    @pl.when(e == 0)
    def _():
        xg_ref[...] = jnp.zeros(xg_ref.shape, F32)
        acc_ref[t_rows:t_rows + ROW_GROUP, :] = jnp.zeros((ROW_GROUP, acc_ref.shape[1]), F32)

        def shared(c, _):
            rows = pl.ds(pl.multiple_of(c * ch, ch), ch)
            acc_ref[rows, :] = ffn(h_ref[rows, :].astype(BF16), wgs_ref[...], wus_ref[...], wds_ref[...])
            return 0

        lax.fori_loop(0, t_rows // ch, shared, 0)

    n = cnt_ref[t * ne + e]
    base = off_ref[t * ne + e]

    def chunk(c, _):
        r0 = base + c * ch
        rows = jnp.minimum(ch, n - c * ch)
        groups = lax.shift_right_logical(rows + (ROW_GROUP - 1), ROW_GROUP.bit_length() - 1)

        def gather(g, _):
            r = r0 + g * ROW_GROUP
            picked = [h_ref[pl.ds(tok_ref[r + j], 1), :] for j in range(ROW_GROUP)]
            xg_ref[pl.ds(pl.multiple_of(g * ROW_GROUP, ROW_GROUP), ROW_GROUP), :] = jnp.concatenate(picked, axis=0)
            return 0

        lax.fori_loop(0, groups, gather, 0)
        y_ref[...] = ffn(xg_ref[...].astype(BF16), wg_ref[...], wu_ref[...], wd_ref[...])

        def scatter(g, _):
            y8 = y_ref[pl.ds(pl.multiple_of(g * ROW_GROUP, ROW_GROUP), ROW_GROUP), :]
            dst, gate = [], []
            for j in range(ROW_GROUP):
                r = g * ROW_GROUP + j
                ok = r < rows
                dst.append(jnp.where(ok, tok_ref[r0 + r], t_rows + j))
                gate.append(jnp.where(ok, gate_ref[r0 + r], 0.0))
            new = [acc_ref[pl.ds(dst[j], 1), :] + gate[j] * y8[j:j + 1, :] for j in range(ROW_GROUP)]
            for j in range(ROW_GROUP):
                acc_ref[pl.ds(dst[j], 1), :] = new[j]
            return 0

        lax.fori_loop(0, groups, scatter, 0)
        return 0

    lax.fori_loop(0, (n + ch - 1) // ch, chunk, 0)

    @pl.when(e == ne - 1)
    def _():
        o_ref[...] = x_ref[...] + g5_ref[...] * acc_ref[0:t_rows, :]


def _moe(h, x, gate5, idx6, g6, wg, wu, wd, wgs, wus, wds, group_rows):
    r, d = h.shape
    t_rows = _row_tile(min(r, group_rows), MOE_TOKEN_TILE)
    ch = min(MOE_ROW_CHUNK, t_rows)
    nt = r // t_rows
    per = t_rows * TOP_K
    tpg = max(group_rows // t_rows, 1)
    e = idx6.reshape(nt, per)
    order = jnp.argsort(e, axis=-1, stable=True)
    per_pad = -(-(per + ROW_GROUP) // 1024) * 1024
    padl = lambda a: jnp.pad(a, ((0, 0), (0, per_pad - per))).reshape(-1)
    tok = padl((order // TOP_K).astype(I32))
    gs = padl(jnp.take_along_axis(g6.reshape(nt, per), order, axis=-1))
    cnt = jnp.sum((e[:, :, None] == jnp.arange(N_EXPERTS, dtype=I32)).astype(I32), axis=1)
    off = (jnp.cumsum(cnt, axis=-1) - cnt).astype(I32).reshape(-1)
    cnt = cnt.reshape(-1)
    if gate5.ndim == 3:
        g5spec = pl.BlockSpec((None, 1, d), lambda ti, ei, *_: (ti // tpg, 0, 0))
    else:
        g5spec = pl.BlockSpec((t_rows, d), lambda ti, ei, *_: (ti, 0))
    tile = pl.BlockSpec((t_rows, d), lambda ti, ei, *_: (ti, 0))
    tile_in = pl.BlockSpec((t_rows, d), lambda ti, ei, *_: (ti, 0), pipeline_mode=pl.Buffered(1))
    smem =pl.BlockSpec((per_pad,), lambda ti, ei, *_: (ti,), memory_space=pltpu.SMEM)
    expert = lambda a: pl.BlockSpec((None,) + a.shape[1:], lambda ti, ei, *_: (ei, 0, 0))
    const = lambda a: pl.BlockSpec(a.shape, lambda ti, ei, *_: (0,) * a.ndim)
    return pl.pallas_call(
        functools.partial(_moe_kernel, t_rows=t_rows, ch=ch),
        grid_spec=pltpu.PrefetchScalarGridSpec(
            num_scalar_prefetch=2, grid=(nt, N_EXPERTS),
            in_specs=[smem, smem, tile_in, tile_in, g5spec, expert(wg), expert(wu), expert(wd),
                      const(wgs), const(wus), const(wds)],
            out_specs=tile,
            scratch_shapes=[pltpu.VMEM((t_rows + ROW_GROUP, d), F32), pltpu.VMEM((ch, d), F32),
                            pltpu.VMEM((ch, d), F32)]),
        out_shape=jax.ShapeDtypeStruct((r, d), F32),
        compiler_params=_cparams(("arbitrary", "arbitrary")), name="moe",
    )(cnt, off, tok, gs, h, x, gate5, wg, wu, wd, wgs, wus, wds)


def _dil_prompt_kernel(q_ref, k_ref, v_ref, tbl_ref, o_ref, qs_ref, m_ref, l_ref, acc_ref, *, tq, nd, n_pairs):
    i = pl.program_id(2)
    lo_half = lax.broadcasted_iota(I32, (tq, LANE), 1) < HEAD_DIM
    for hp in range(n_pairs):
        qs_ref[hp] = _stack_pair(q_ref[:, hp * LANE:(hp + 1) * LANE], lo_half)
    _flash_init(m_ref, l_ref, acc_ref)

    def attend(j, _):
        rows = pl.ds(pl.multiple_of(j * tq, tq), tq)
        b = tbl_ref[i - j]
        bias2 = jnp.concatenate([b, b], axis=0)
        for hp in range(n_pairs):
            cols = slice(hp * LANE, (hp + 1) * LANE)
            _flash_step(qs_ref[hp], k_ref[rows, cols], v_ref[rows, cols], bias2, m_ref, l_ref, acc_ref, hp)
        return 0

    lax.fori_loop(jnp.maximum(i - (nd - 1), 0), i + 1, attend, 0)
    for hp in range(n_pairs):
        o_ref[:, hp * LANE:(hp + 1) * LANE] = _flash_finish(l_ref, acc_ref, hp, lo_half, tq).astype(o_ref.dtype)


def _dilated_prompt(q, k, v):
    b, s, w = q.shape
    tq = min(DIL_Q_TILE, s)
    nd = max(wd for wd, _ in DILATED_PATTERNS) // tq + 1
    dd = np.arange(nd)[:, None, None]
    rel = dd * tq + np.arange(tq)[None, :, None] - np.arange(tq)[None, None, :]
    tbl = jnp.asarray(_mult_bias(rel))
    n_pairs = 2
    gw = n_pairs * LANE
    state = pltpu.VMEM((n_pairs, 2 * tq, LANE), F32)
    return pl.pallas_call(
        functools.partial(_dil_prompt_kernel, tq=tq, nd=nd, n_pairs=n_pairs), grid=(b, w // gw, s // tq),
        in_specs=[pl.BlockSpec((None, tq, gw), lambda bi, hp, i: (bi, i, hp)),
                  pl.BlockSpec((None, s, gw), lambda bi, hp, i: (bi, 0, hp)),
                  pl.BlockSpec((None, s, gw), lambda bi, hp, i: (bi, 0, hp)),
                  pl.BlockSpec((nd, tq, tq), lambda bi, hp, i: (0, 0, 0))],
        out_specs=pl.BlockSpec((None, tq, gw), lambda bi, hp, i: (bi, i, hp)),
        out_shape=jax.ShapeDtypeStruct((b, s, w), BF16),
        scratch_shapes=[pltpu.VMEM((n_pairs, 2 * tq, LANE), BF16), state, state, state],
        compiler_params=_cparams(("arbitrary", "arbitrary", "arbitrary")), name="dilated_prompt",
    )(q, k, v, tbl)


def _dil_sample_kernel(qbd_ref, kc_ref, kn_ref, vc_ref, vn_ref, tc_ref, tn_ref, o_ref, *, n_new):
    qbd = qbd_ref[...]
    scale = HEAD_DIM ** -0.5
    sc = _dot(qbd, kc_ref[...].astype(BF16)) * scale + tc_ref[...]
    sn = _dot_nt(qbd, kn_ref[...].astype(BF16)) * scale + tn_ref[...]
    m = jnp.maximum(jnp.max(sc, axis=1, keepdims=True), jnp.max(sn, axis=1, keepdims=True))
    pc = jnp.exp(sc - m)
    pn = jnp.exp(sn - m)
    l = jnp.sum(pc, axis=1, keepdims=True) + jnp.sum(pn, axis=1, keepdims=True)
    acc = _dot_nt(pc.astype(BF16), vc_ref[...].astype(BF16)) + _dot(pn.astype(BF16), vn_ref[...].astype(BF16))
    o_ref[...] = _block_diag_rows(acc / l, N_HEADS_C, n_new)


def _dilated_sample(q, k_new, v_new, cache_k, cache_v):
    db, ds, w = q.shape
    wb = cache_k.shape[1]
    rows = N_HEADS_C * ds
    qh = q.reshape(db, ds, N_HEADS_C, HEAD_DIM).transpose(0, 2, 1, 3)
    eye = jnp.eye(N_HEADS_C, dtype=q.dtype)
    qbd = (qh[:, :, :, None, :] * eye[None, :, None, :, None]).reshape(db, rows, w)
    kt = jnp.transpose(cache_k, (0, 2, 3, 1)).reshape(db, w, wb)
    vt = jnp.transpose(cache_v, (0, 2, 3, 1)).reshape(db, w, wb)
    pad = lambda a: jnp.pad(a, ((0, 0), (0, LANE - ds), (0, 0)))
    rel = (wb + np.arange(ds)[:, None]) - np.arange(wb + LANE)[None, :]
    bias = _mult_bias(rel)
    bias[:, wb + ds:] = NEG
    tbl = np.tile(bias, (N_HEADS_C, 1))
    per_b = lambda *shape: pl.BlockSpec((None,) + shape, lambda b: (b,) + (0,) * len(shape))
    const = lambda a: pl.BlockSpec(a.shape, lambda b: (0,) * a.ndim)
    tc, tn = jnp.asarray(tbl[:, :wb]), jnp.asarray(tbl[:, wb:])
    return pl.pallas_call(
        functools.partial(_dil_sample_kernel, n_new=ds), grid=(db,),
        in_specs=[per_b(rows, w), per_b(w, wb), per_b(LANE, w), per_b(w, wb), per_b(LANE, w), const(tc), const(tn)],
        out_specs=per_b(ds, w),
        out_shape=jax.ShapeDtypeStruct((db, ds, w), F32),
        compiler_params=_cparams(("arbitrary",)), name="dilated_sample",
    )(qbd, kt, pad(k_new), vt, pad(v_new), tc, tn)


def kernel(x_prompt, x_sample, c_prompt, c_sample, cache_a_k, cache_a_v, cache_a_idx_k, state_conv, cache_c_k, cache_c_v, page_table, norm_mix_g, norm_ffn_g, w_ada, b_ada, w_in_even, w_out_even, a_q_norm_g, a_k_norm_g, b_conv_w, b_conv_b, b_ln_g, b_ln_b, w_in_odd, w_out_odd, c_q_norm_g, c_k_norm_g, w_router, b_router, w_gate_e, w_up_e, w_down_e, w_gate_s, w_up_s, w_down_s):
    bp, sp, d = x_prompt.shape
    bs, ss, _ = x_sample.shape
    depth = w_ada.shape[0]
    past = page_table.shape[1] * cache_a_k.shape[2]
    win_buf = cache_c_k.shape[2]
    rp, rs = bp * sp, bs * ss

    tab_p = _rope_tables(jnp.arange(sp, dtype=I32))
    tab_s = _rope_tables(jnp.tile(past + jnp.arange(ss, dtype=I32), bs))
    row = lambda a: a.reshape(1, -1)
    gain2 = lambda g: jnp.tile(g, 2).reshape(1, LANE)

    xp = x_prompt.reshape(rp, d)
    xs = x_sample.reshape(rs, d)
    c_all = jnp.concatenate([c_prompt, c_sample], axis=0)
    c_all = jnp.pad(c_all, ((0, -c_all.shape[0] % 8), (0, 0)))

    ak_p, av_p, ai_p, ak_s, av_s, ai_s, cv_p, cv_s = [], [], [], [], [], [], [], []
    ck_p, cw_p, ck_s, cw_s = [], [], [], []
    for layer in range(depth):
        mod = _adaln(c_all, w_ada[layer], b_ada[layer])
        mod_p = [m.reshape(bp, 1, d) for m in jnp.split(mod[:bp], 6, axis=-1)]
        mod_s = [jnp.repeat(m, ss, axis=0) for m in jnp.split(mod[bp:bp + bs], 6, axis=-1)]
        gmix, gffn = row(norm_mix_g[layer]), row(norm_ffn_g[layer])
        if layer % 2 == 0:
            e = layer // 2
            w_in = _perm_w_even(w_in_even[e])
            w_out = w_out_even[e].astype(BF16)
            qg, kg = gain2(a_q_norm_g[e]), gain2(a_k_norm_g[e])
            outs = [(A_W, BF16), (A_W, F32), (A_W, BF16), (A_W, F32), (A_W, BF16), (2 * LANE, BF16),
                    (LANE, F32), (LANE, BF16), (LANE, F32), (CONV_CH, F32)]
            q, k, kb, v, vb, qi, ki2, ki2b, wi, u = _project(
                _proj_even_kernel, xp, gmix, mod_p[0], mod_p[1], w_in, tab_p, qg, kg, sp, outs)
            b3 = lambda a: a.reshape(bp, sp, a.shape[-1])
            oa = _dsa_prompt(b3(q), b3(qi), b3(wi), b3(ki2b), b3(kb), b3(vb)).reshape(rp, A_W)
            u3 = b3(u)
            ob = _conv_module(u3, jnp.zeros((bp, CONV_WIDTH - 1, CONV_CH), F32),
                              b_conv_w[e], b_conv_b[e], b_ln_g[e], b_ln_b[e]).reshape(rp, CONV_CH)
            lhs_p = [oa, ob]
            ak_p.append(k.reshape(bp, sp, N_HEADS_A, HEAD_DIM))
            av_p.append(v.reshape(bp, sp, N_HEADS_A, HEAD_DIM))
            ai_p.append(ki2[:, :IDX_DIM].reshape(bp, sp, IDX_DIM))
            xpad = jnp.concatenate([jnp.zeros((bp, CONV_WIDTH - 1, CONV_CH), F32), u3], axis=1)
            cv_p.append(xpad[:, -(CONV_WIDTH - 1):])
            q, k, kb, v, vb, qi, ki2, ki2b, wi, u = _project(
                _proj_even_kernel, xs, gmix, mod_s[0], mod_s[1], w_in, tab_s, qg, kg, rs, outs)
            s3 = lambda a: a.reshape(bs, ss, a.shape[-1])
            oa = _dsa_sample(s3(q), s3(k), s3(v), s3(qi), s3(ki2)[:, :, :IDX_DIM], s3(wi)[:, :, :N_IDX_HEADS],
                             cache_a_k[e], cache_a_v[e], cache_a_idx_k[e], page_table)
            u3 = s3(u)
            ob = _conv_module(u3, state_conv[e], b_conv_w[e], b_conv_b[e], b_ln_g[e], b_ln_b[e])
            lhs_s = [oa.reshape(rs, A_W).astype(BF16), ob.reshape(rs, CONV_CH)]
            ak_s.append(k.reshape(bs, ss, N_HEADS_A, HEAD_DIM))
            av_s.append(v.reshape(bs, ss, N_HEADS_A, HEAD_DIM))
            ai_s.append(ki2[:, :IDX_DIM].reshape(bs, ss, IDX_DIM))
            cv_s.append(jnp.concatenate([state_conv[e], u3], axis=1)[:, -(CONV_WIDTH - 1):])
            ws = [w_out[:A_W], w_out[A_W:]]
        else:
            o = layer // 2
            w_in = w_in_odd[o].astype(BF16)
            qg, kg = gain2(c_q_norm_g[o]), gain2(c_k_norm_g[o])
            outs = [(C_W, BF16), (C_W, F32), (C_W, BF16), (C_W, F32), (C_W, BF16)]
            q, k, kb, v, vb = _project(_proj_odd_kernel, xp, gmix, mod_p[0], mod_p[1], w_in, tab_p, qg, kg, sp, outs)
            b3 = lambda a: a.reshape(bp, sp, a.shape[-1])
            lhs_p = [_dilated_prompt(b3(q), b3(kb), b3(vb)).reshape(rp, C_W)]
            k4, v4 = k.reshape(bp, sp, N_HEADS_C, HEAD_DIM), v.reshape(bp, sp, N_HEADS_C, HEAD_DIM)
            padt = max(win_buf - sp, 0)
            tail = lambda a: jnp.pad(a, ((0, 0), (padt, 0), (0, 0), (0, 0)))[:, -win_buf:]
            ck_p.append(tail(k4))
            cw_p.append(tail(v4))
            q, k, kb, v, vb = _project(_proj_odd_kernel, xs, gmix, mod_s[0], mod_s[1], w_in, tab_s, qg, kg, rs, outs)
            s3 = lambda a: a.reshape(bs, ss, a.shape[-1])
            att = _dilated_sample(s3(q), s3(k), s3(v), cache_c_k[o], cache_c_v[o])
            lhs_s = [att.reshape(rs, C_W).astype(BF16)]
            k4, v4 = k.reshape(bs, ss, N_HEADS_C, HEAD_DIM), v.reshape(bs, ss, N_HEADS_C, HEAD_DIM)
            ck_s.append(jnp.concatenate([cache_c_k[o], k4], axis=1)[:, -win_buf:])
            cw_s.append(jnp.concatenate([cache_c_v[o], v4], axis=1)[:, -win_buf:])
            ws = [w_out_odd[o].astype(BF16)]

        wg, wu, wd = w_gate_e[layer].astype(BF16), w_up_e[layer].astype(BF16), w_down_e[layer].astype(BF16)
        wgs, wus, wds = w_gate_s[layer].astype(BF16), w_up_s[layer].astype(BF16), w_down_s[layer].astype(BF16)
        xn, h, idx, gate = _outproj_router(lhs_p, ws, xp, mod_p[2], gffn, mod_p[3], mod_p[4],
                                           w_router[layer], b_router[layer], sp)
        xp = _moe(h, xn, mod_p[5], idx[:, :TOP_K], gate[:, :TOP_K], wg, wu, wd, wgs, wus, wds, sp)
        xn, h, idx, gate = _outproj_router(lhs_s, ws, xs, mod_s[2], gffn, mod_s[3], mod_s[4],
                                           w_router[layer], b_router[layer], rs)
        xs = _moe(h, xn, mod_s[5], idx[:, :TOP_K], gate[:, :TOP_K], wg, wu, wd, wgs, wus, wds, rs)

    return (xp.reshape(bp, sp, d), xs.reshape(bs, ss, d),
            jnp.stack(ak_p), jnp.stack(av_p), jnp.stack(ai_p),
            jnp.stack(ak_s), jnp.stack(av_s), jnp.stack(ai_s),
            jnp.stack(cv_p), jnp.stack(cv_s),
            jnp.stack(ck_p), jnp.stack(cw_p), jnp.stack(ck_s), jnp.stack(cw_s))
```
